```python
import math
import jax, jax.numpy as jnp
from jax import lax
import numpy as np

D_MODEL = 1024
BATCH = 8
SEQ = 2048
DEPTH = 4
DEC_BATCH = 128
DEC_SEQ = 1
PAST_LEN = 8192
PAGE_SIZE = 128

HEAD_DIM = 64
N_HEADS_DIFF = D_MODEL // (2 * HEAD_DIM)
N_KV_DIFF = N_HEADS_DIFF // 2
G_DIFF = N_HEADS_DIFF // N_KV_DIFF
DV_DIFF = 2 * HEAD_DIM
N_HEADS_SWA = D_MODEL // HEAD_DIM
N_KV_SWA = max(1, N_HEADS_SWA // 8)
G_SWA = N_HEADS_SWA // N_KV_SWA
WINDOW = 128
NUM_BUCKETS = 32
MAX_DISTANCE = 128
N_BIAS_HEADS = N_HEADS_SWA
D_FF = -(-8 * D_MODEL // (3 * 256)) * 256
Q_BLOCK = 128
N_DIFF_LAYERS = (DEPTH + 1) // 2
N_SWA_LAYERS = DEPTH // 2
RMS_EPS = 1e-6
NEG_INF = -1e30

Q_DIFF = N_HEADS_DIFF * 2 * HEAD_DIM
K_DIFF = N_KV_DIFF * 2 * HEAD_DIM
V_DIFF = N_KV_DIFF * DV_DIFF
QKV_DIFF = Q_DIFF + K_DIFF + V_DIFF
O_DIFF = N_HEADS_DIFF * DV_DIFF
Q_SWA = N_HEADS_SWA * HEAD_DIM
K_SWA = N_KV_SWA * HEAD_DIM
QKV_SWA = Q_SWA + 2 * K_SWA
O_SWA = Q_SWA

F32 = jnp.float32

kernel_name = 'diffattn_swa_sink_hybrid_step'


def _rms_norm(x, g):
    xf = x.astype(F32)
    y = xf * lax.rsqrt(jnp.mean(xf * xf, axis=-1, keepdims=True) + RMS_EPS)
    return (y * g.astype(F32)).astype(x.dtype)


def _rel_bucket(dist):
    n = jnp.maximum(dist, 0)
    max_exact = NUM_BUCKETS // 2
    nf = jnp.maximum(n, 1).astype(F32)
    large = max_exact + (jnp.log(nf / max_exact) / math.log(MAX_DISTANCE / max_exact)
                         * (NUM_BUCKETS - max_exact)).astype(jnp.int32)
    large = jnp.minimum(large, NUM_BUCKETS - 1)
    return jnp.where(n < max_exact, n, large)


def _bias_diff(dist, table):
    b = table.astype(F32)[_rel_bucket(dist)]
    b = b.reshape(dist.shape + (N_KV_DIFF, G_DIFF, 2))
    return jnp.moveaxis(b, (0, 1), (-2, -1))


def _bias_swa(dist, table):
    b = table.astype(F32)[_rel_bucket(dist)]
    b = b.reshape(dist.shape + (N_KV_SWA, G_SWA))
    return jnp.moveaxis(b, (0, 1), (-2, -1))


def _diff_qkv(h, w_qkv):
    b, t, _ = h.shape
    qkv = h @ w_qkv
    q = qkv[..., :Q_DIFF].reshape(b, t, N_KV_DIFF, G_DIFF, 2, HEAD_DIM) * (HEAD_DIM ** -0.5)
    k = qkv[..., Q_DIFF:Q_DIFF + K_DIFF].reshape(b, t, N_KV_DIFF, 2, HEAD_DIM)
    v = qkv[..., Q_DIFF + K_DIFF:].reshape(b, t, N_KV_DIFF, DV_DIFF)
    return q, k, v


def _diff_lambda(lam_p, lam_init):
    lp = lam_p.astype(F32)
    return jnp.exp(jnp.sum(lp[0] * lp[1])) - jnp.exp(jnp.sum(lp[2] * lp[3])) + lam_init


def _diff_combine(o, lam, lam_init, subln, dtype):
    d = o[..., 0, :] - lam * o[..., 1, :]
    d = d * lax.rsqrt(jnp.mean(d * d, axis=-1, keepdims=True) + RMS_EPS) * subln.astype(F32)
    d = d * (1.0 - lam_init)
    return d.reshape(d.shape[:2] + (-1,)).astype(dtype)


def _diff_attn_prompt(h, w_qkv, w_o, lam_p, subln, lam_init, table):
    b, s_len, _ = h.shape
    q, k, v = _diff_qkv(h, w_qkv)
    qf, kf, vf = q.astype(F32), k.astype(F32), v.astype(F32)
    nqb = s_len // Q_BLOCK
    qb = jnp.moveaxis(qf.reshape((b, nqb, Q_BLOCK) + qf.shape[2:]), 1, 0)
    kpos = jnp.arange(s_len)

    def one_block(args):
        q_blk, start = args
        dist = (start + jnp.arange(Q_BLOCK))[:, None] - kpos[None, :]
        sc = jnp.einsum('btkgmd,bskmd->bkgmts', q_blk, kf) + _bias_diff(dist, table)
        sc = jnp.where(dist >= 0, sc, NEG_INF)
        p = jax.nn.softmax(sc, axis=-1)
        return jnp.einsum('bkgmts,bskd->btkgmd', p, vf)

    o = lax.map(one_block, (qb, jnp.arange(nqb) * Q_BLOCK))
    o = jnp.moveaxis(o, 0, 1).reshape((b, s_len, N_KV_DIFF, G_DIFF, 2, DV_DIFF))
    lam = _diff_lambda(lam_p, lam_init)
    out = _diff_combine(o, lam, lam_init, subln, h.dtype) @ w_o
    return out, k.reshape(b, s_len, N_KV_DIFF, 2 * HEAD_DIM), v


def _online_update(carry, sc, v):
    m, l, acc = carry
    m_new = jnp.maximum(m, jnp.max(sc, axis=-1))
    alpha = jnp.exp(m - m_new)
    p = jnp.exp(sc - m_new[..., None])
    l = l * alpha + jnp.sum(p, axis=-1)
    acc = acc * alpha[..., None] + jnp.einsum('bkgmts,bskd->bkgmtd', p, v)
    return (m_new, l, acc)


def _diff_attn_sample(h, cache_k, cache_v, li, page_table, w_qkv, w_o, lam_p, subln, lam_init, table):
    b, t, _ = h.shape
    q, k, v = _diff_qkv(h, w_qkv)
    qf, kf, vf = q.astype(F32), k.astype(F32), v.astype(F32)
    page = cache_k.shape[2]
    n_pages = page_table.shape[1]
    past = n_pages * page
    qpos = past + jnp.arange(t)
    stat = (b, N_KV_DIFF, G_DIFF, 2, t)
    init = (jnp.full(stat, NEG_INF, F32), jnp.zeros(stat, F32), jnp.zeros(stat + (DV_DIFF,), F32))

    def page_step(carry, p):
        phys = page_table[:, p]
        kp = cache_k[phys, li].astype(F32).reshape(b, page, N_KV_DIFF, 2, HEAD_DIM)
        vp = cache_v[phys, li].astype(F32)
        dist = qpos[:, None] - (p * page + jnp.arange(page))[None, :]
        sc = jnp.einsum('btkgmd,bskmd->bkgmts', qf, kp) + _bias_diff(dist, table)
        return _online_update(carry, sc, vp), None

    carry, _ = lax.scan(page_step, init, jnp.arange(n_pages))
    dist = jnp.arange(t)[:, None] - jnp.arange(t)[None, :]
    sc = jnp.einsum('btkgmd,bskmd->bkgmts', qf, kf) + _bias_diff(dist, table)
    sc = jnp.where(dist >= 0, sc, NEG_INF)
    _, l, acc = _online_update(carry, sc, vf)
    o = jnp.transpose(acc / l[..., None], (0, 4, 1, 2, 3, 5))
    lam = _diff_lambda(lam_p, lam_init)
    out = _diff_combine(o, lam, lam_init, subln, h.dtype) @ w_o
    return out, k.reshape(b, t, N_KV_DIFF, 2 * HEAD_DIM), v


def _swa_qkv(h, w_qkv, b_qkv):
    b, t, _ = h.shape
    qkv = h @ w_qkv + b_qkv
    q = qkv[..., :Q_SWA].reshape(b, t, N_KV_SWA, G_SWA, HEAD_DIM) * (HEAD_DIM ** -0.5)
    k = qkv[..., Q_SWA:Q_SWA + K_SWA].reshape(b, t, N_KV_SWA, HEAD_DIM)
    v = qkv[..., Q_SWA + K_SWA:].reshape(b, t, N_KV_SWA, HEAD_DIM)
    return q, k, v


def _sink_attend(q, k, v, bias, valid, sinks):
    sc = jnp.einsum('ntkgd,nskd->nkgts', q.astype(F32), k.astype(F32)) + bias
    sc = jnp.where(valid, sc, NEG_INF)
    sink = sinks.astype(F32).reshape(1, N_KV_SWA, G_SWA, 1)
    m = jnp.maximum(jnp.max(sc, axis=-1), sink)
    p = jnp.exp(sc - m[..., None])
    denom = jnp.sum(p, axis=-1) + jnp.exp(sink - m)
    o = jnp.einsum('nkgts,nskd->nkgtd', p, v.astype(F32)) / denom[..., None]
    return jnp.transpose(o, (0, 3, 1, 2, 4))


def _swa_prompt(h, w_qkv, b_qkv, w_o, sinks, table):
    b, s_len, _ = h.shape
    q, k, v = _swa_qkv(h, w_qkv, b_qkv)
    nb = s_len // WINDOW

    def band(x):
        xb = x.reshape(b, nb, WINDOW, N_KV_SWA, HEAD_DIM)
        prev = jnp.concatenate([jnp.zeros_like(xb[:, :1]), xb[:, :-1]], axis=1)
        return jnp.concatenate([prev, xb], axis=2).reshape(b * nb, 2 * WINDOW, N_KV_SWA, HEAD_DIM)

    qi = jnp.arange(WINDOW)
    si = jnp.arange(2 * WINDOW)
    dist = qi[:, None] + WINDOW - si[None, :]
    local = (dist >= 0) & (dist < WINDOW)
    has_prev = (jnp.arange(nb)[:, None, None] > 0) | (si[None, None, :] >= WINDOW)
    valid = jnp.broadcast_to(local[None] & has_prev, (b, nb, WINDOW, 2 * WINDOW))
    valid = valid.reshape(b * nb, 1, 1, WINDOW, 2 * WINDOW)
    o = _sink_attend(q.reshape(b * nb, WINDOW, N_KV_SWA, G_SWA, HEAD_DIM), band(k), band(v),
                     _bias_swa(dist, table), valid, sinks)
    out = o.reshape(b, s_len, O_SWA).astype(h.dtype) @ w_o
    buf = min(WINDOW, s_len)
    return out, k[:, s_len - buf:], v[:, s_len - buf:]


def _swa_sample(h, state_k, state_v, li, w_qkv, b_qkv, w_o, sinks, table):
    b, t, _ = h.shape
    q, k, v = _swa_qkv(h, w_qkv, b_qkv)
    kk = jnp.concatenate([state_k[:, li], k], axis=1)
    vv = jnp.concatenate([state_v[:, li], v], axis=1)
    buf = state_k.shape[2]
    kpos = jnp.concatenate([jnp.arange(buf) - buf, jnp.arange(t)])
    dist = jnp.arange(t)[:, None] - kpos[None, :]
    valid = ((dist >= 0) & (dist < WINDOW))[None, None, None]
    o = _sink_attend(q, kk, vv, _bias_swa(dist, table), valid, sinks)
    out = o.reshape(b, t, O_SWA).astype(h.dtype) @ w_o
    return out, kk[:, t:], vv[:, t:]


def _swiglu(h, w_gate, w_up, w_down):
    return (jax.nn.silu(h @ w_gate) * (h @ w_up)) @ w_down


def setup_inputs(seed: int = 0) -> dict:
    key = jax.random.key(seed)
    ks = jax.random.split(key, 22)
    n_pages = PAST_LEN // PAGE_SIZE
    n_used = DEC_BATCH * n_pages
    n_pool = n_used + n_used // 4
    buf = min(WINDOW, PAST_LEN)

    def nrm(k, shape, scale):
        return jax.random.normal(k, shape, F32) * scale

    d_in = D_MODEL ** -0.5
    page_table = jax.random.permutation(ks[6], n_pool)[:n_used].reshape(DEC_BATCH, n_pages).astype(jnp.int32)
    return {
        'x_prompt': nrm(ks[0], (BATCH, SEQ, D_MODEL), 1.0),
        'x_sample': nrm(ks[1], (DEC_BATCH, DEC_SEQ, D_MODEL), 1.0),
        'cache_k_diff': nrm(ks[2], (n_pool, N_DIFF_LAYERS, PAGE_SIZE, N_KV_DIFF, 2 * HEAD_DIM), 1.0),
        'cache_v_diff': nrm(ks[3], (n_pool, N_DIFF_LAYERS, PAGE_SIZE, N_KV_DIFF, DV_DIFF), 1.0),
        'state_k_swa': nrm(ks[4], (DEC_BATCH, N_SWA_LAYERS, buf, N_KV_SWA, HEAD_DIM), 1.0),
        'state_v_swa': nrm(ks[5], (DEC_BATCH, N_SWA_LAYERS, buf, N_KV_SWA, HEAD_DIM), 1.0),
        'page_table': page_table,
        'rel_bias': nrm(ks[7], (NUM_BUCKETS, N_BIAS_HEADS), 0.3),
        'g_mix': 1.0 + nrm(ks[8], (DEPTH, D_MODEL), 0.02),
        'g_ffn': 1.0 + nrm(ks[9], (DEPTH, D_MODEL), 0.02),
        'g_final': 1.0 + nrm(ks[10], (D_MODEL,), 0.02),
        'w_qkv_diff': nrm(ks[11], (N_DIFF_LAYERS, D_MODEL, QKV_DIFF), d_in),
        'w_o_diff': nrm(ks[12], (N_DIFF_LAYERS, O_DIFF, D_MODEL), O_DIFF ** -0.5),
        'lambda_diff': nrm(ks[13], (N_DIFF_LAYERS, 4, HEAD_DIM), 0.1),
        'subln_diff': 1.0 + nrm(ks[14], (N_DIFF_LAYERS, DV_DIFF), 0.02),
        'w_qkv_swa': nrm(ks[15], (N_SWA_LAYERS, D_MODEL, QKV_SWA), d_in),
        'b_qkv_swa': nrm(ks[16], (N_SWA_LAYERS, QKV_SWA), 0.02),
        'w_o_swa': nrm(ks[17], (N_SWA_LAYERS, O_SWA, D_MODEL), O_SWA ** -0.5),
        'sinks_swa': nrm(ks[18], (N_SWA_LAYERS, N_HEADS_SWA), 0.5),
        'w_gate': nrm(ks[19], (DEPTH, D_MODEL, D_FF), d_in),
        'w_up': nrm(ks[20], (DEPTH, D_MODEL, D_FF), d_in),
        'w_down': nrm(ks[21], (DEPTH, D_FF, D_MODEL), D_FF ** -0.5),
    }


def reference(x_prompt, x_sample, cache_k_diff, cache_v_diff, state_k_swa, state_v_swa, page_table,
              rel_bias, g_mix, g_ffn, g_final, w_qkv_diff, w_o_diff, lambda_diff, subln_diff,
              w_qkv_swa, b_qkv_swa, w_o_swa, sinks_swa, w_gate, w_up, w_down):
    xp, xs = x_prompt, x_sample
    kdp, vdp, kds, vds = [], [], [], []
    ksp, vsp, kss, vss = [], [], [], []
    for i in range(DEPTH):
        li = i // 2
        hp = _rms_norm(xp, g_mix[i])
        hs = _rms_norm(xs, g_mix[i])
        if i % 2 == 0:
            lam_init = 0.8 - 0.6 * math.exp(-0.3 * i)
            op, kp, vp = _diff_attn_prompt(hp, w_qkv_diff[li], w_o_diff[li], lambda_diff[li],
                                           subln_diff[li], lam_init, rel_bias)
            os_, ks_, vs_ = _diff_attn_sample(hs, cache_k_diff, cache_v_diff, li, page_table,
                                              w_qkv_diff[li], w_o_diff[li], lambda_diff[li],
                                              subln_diff[li], lam_init, rel_bias)
            kdp.append(kp); vdp.append(vp); kds.append(ks_); vds.append(vs_)
        else:
            op, kp, vp = _swa_prompt(hp, w_qkv_swa[li], b_qkv_swa[li], w_o_swa[li], sinks_swa[li], rel_bias)
            os_, ks_, vs_ = _swa_sample(hs, state_k_swa, state_v_swa, li, w_qkv_swa[li], b_qkv_swa[li],
                                        w_o_swa[li], sinks_swa[li], rel_bias)
            ksp.append(kp); vsp.append(vp); kss.append(ks_); vss.append(vs_)
        xp = xp + op
        xs = xs + os_
        xp = xp + _swiglu(_rms_norm(xp, g_ffn[i]), w_gate[i], w_up[i], w_down[i])
        xs = xs + _swiglu(_rms_norm(xs, g_ffn[i]), w_gate[i], w_up[i], w_down[i])
    y_prompt = _rms_norm(xp, g_final)
    y_sample = _rms_norm(xs, g_final)
    new_k_diff_prompt = jnp.stack(kdp, axis=1)
    new_v_diff_prompt = jnp.stack(vdp, axis=1)
    new_k_diff_sample = jnp.stack(kds, axis=1)
    new_v_diff_sample = jnp.stack(vds, axis=1)
    new_k_swa_prompt = jnp.stack(ksp, axis=1)
    new_v_swa_prompt = jnp.stack(vsp, axis=1)
    new_k_swa_sample = jnp.stack(kss, axis=1)
    new_v_swa_sample = jnp.stack(vss, axis=1)
    return (y_prompt, y_sample, new_k_diff_prompt, new_v_diff_prompt, new_k_diff_sample,
            new_v_diff_sample, new_k_swa_prompt, new_v_swa_prompt, new_k_swa_sample, new_v_swa_sample)
```

```python
import functools
import math

import numpy as np
import jax
import jax.numpy as jnp
from jax import lax
from jax.experimental import pallas as pl
from jax.experimental.pallas import tpu as pltpu

F32 = jnp.float32
BF16 = jnp.bfloat16

HEAD_DIM = 64
WINDOW = 128
NUM_BUCKETS = 32
MAX_DISTANCE = 128
RMS_EPS = 1e-6
NEG_INF = -1e30
BLK = 128
LANES = 128
VMEM_LIMIT = 56 * 1024 * 1024

_NT = (((1,), (1,)), ((), ()))


def _rms(x, g):
    return x * lax.rsqrt(jnp.mean(x * x, axis=-1, keepdims=True) + RMS_EPS) * g


def _cparams(sem):
    return pltpu.CompilerParams(dimension_semantics=sem, vmem_limit_bytes=VMEM_LIMIT)


def _bucket_table():
    dist = np.arange(BLK)[:, None] + BLK - np.arange(2 * BLK)[None, :]
    n = np.maximum(dist, 0)
    max_exact = NUM_BUCKETS // 2
    nf = np.maximum(n, 1).astype(np.float32)
    large = max_exact + (np.log(nf / np.float32(max_exact)) / np.float32(math.log(MAX_DISTANCE / max_exact))
                         * np.float32(NUM_BUCKETS - max_exact)).astype(np.int32)
    large = np.minimum(large, NUM_BUCKETS - 1)
    return np.where(n < max_exact, n, large).astype(np.int32)


def _bias_kernel(table_ref, bucket_ref, causal_ref, band_ref):
    col = pl.program_id(0)
    bucket = bucket_ref[...]
    t = jnp.zeros(bucket.shape, F32)
    for b in range(NUM_BUCKETS):
        t = jnp.where(bucket == b, table_ref[b, col], t)
    far = table_ref[NUM_BUCKETS - 1, col]
    dist = (lax.broadcasted_iota(jnp.int32, bucket.shape, 0) + BLK
            - lax.broadcasted_iota(jnp.int32, bucket.shape, 1))
    causal_ref[...] = jnp.where(dist >= 0, t - far, NEG_INF)
    band_ref[...] = jnp.where((dist >= 0) & (dist < WINDOW), t, NEG_INF)


def _bias_tiles(rel_bias):
    n_cols = rel_bias.shape[1]
    bucket = jnp.asarray(_bucket_table())
    shape = jax.ShapeDtypeStruct((n_cols, BLK, 2 * BLK), F32)
    return pl.pallas_call(
        _bias_kernel,
        grid=(n_cols,),
        in_specs=[pl.BlockSpec(memory_space=pltpu.SMEM),
                  pl.BlockSpec((BLK, 2 * BLK), lambda c: (0, 0))],
        out_specs=[pl.BlockSpec((None, BLK, 2 * BLK), lambda c: (c, 0, 0)),
                   pl.BlockSpec((None, BLK, 2 * BLK), lambda c: (c, 0, 0))],
        out_shape=[shape, shape],
        compiler_params=_cparams(("arbitrary",)),
        name="bias_tiles",
    )(rel_bias, bucket)


def _qkv_kernel(x_ref, g_ref, w_ref, b_ref, q_ref, kvf_ref, kvb_ref, h_scr, *, n_q, chunk):
    h_scr[...] = _rms(x_ref[...], g_ref[...]).astype(BF16)
    n_total = w_ref.shape[1]
    for c0 in range(0, n_total, chunk):
        c1 = min(c0 + chunk, n_total)
        acc = jnp.dot(h_scr[...], w_ref[:, c0:c1], preferred_element_type=F32) + b_ref[:, c0:c1]
        if c1 <= n_q:
            q_ref[:, c0:c1] = acc.astype(BF16)
        else:
            kvf_ref[:, c0 - n_q:c1 - n_q] = acc
            kvb_ref[:, c0 - n_q:c1 - n_q] = acc.astype(BF16)


def _qkv_proj(x, g, w, b, n_q, tm):
    m, d = x.shape
    n = w.shape[1]
    n_kv = n - n_q
    chunk = 256
    assert m % tm == 0 and n_q % chunk == 0 and n_kv % chunk == 0
    kern = functools.partial(_qkv_kernel, n_q=n_q, chunk=chunk)
    return pl.pallas_call(
        kern,
        grid=(m // tm,),
        in_specs=[pl.BlockSpec((tm, d), lambda i: (i, 0)),
                  pl.BlockSpec((1, d), lambda i: (0, 0)),
                  pl.BlockSpec((d, n), lambda i: (0, 0)),
                  pl.BlockSpec((1, n), lambda i: (0, 0))],
        out_specs=[pl.BlockSpec((tm, n_q), lambda i: (i, 0)),
                   pl.BlockSpec((tm, n_kv), lambda i: (i, 0)),
                   pl.BlockSpec((tm, n_kv), lambda i: (i, 0))],
        out_shape=[jax.ShapeDtypeStruct((m, n_q), BF16),
                   jax.ShapeDtypeStruct((m, n_kv), F32),
                   jax.ShapeDtypeStruct((m, n_kv), BF16)],
        scratch_shapes=[pltpu.VMEM((tm, d), BF16)],
        compiler_params=_cparams(("arbitrary",)),
        name="qkv_proj",
    )(x, g, w, b)


def _post_kernel(x_ref, o_ref, wo_ref, g_ref, wg_ref, wu_ref, wd_ref, gf_ref, out_ref,
                 h_scr, acc_scr, *, chunk, final):
    x1 = x_ref[...] + jnp.dot(o_ref[...], wo_ref[...], preferred_element_type=F32)
    acc_scr[...] = x1
    h_scr[...] = _rms(x1, g_ref[...]).astype(BF16)
    d_ff = wg_ref.shape[1]
    for c0 in range(0, d_ff, chunk):
        gate = jnp.dot(h_scr[...], wg_ref[:, c0:c0 + chunk], preferred_element_type=F32)
        up = jnp.dot(h_scr[...], wu_ref[:, c0:c0 + chunk], preferred_element_type=F32)
        act = (gate * (1.0 / (1.0 + jnp.exp(-gate))) * up).astype(BF16)
        acc_scr[...] += jnp.dot(act, wd_ref[c0:c0 + chunk, :], preferred_element_type=F32)
    if final:
        out_ref[...] = _rms(acc_scr[...], gf_ref[...])
    else:
        out_ref[...] = acc_scr[...]


def _post_attn(x, o, w_o, g, w_gate, w_up, w_down, g_final, tm, final):
    m, d = x.shape
    d_o = o.shape[1]
    d_ff = w_gate.shape[1]
    chunk = 256
    assert m % tm == 0 and d_ff % chunk == 0
    kern = functools.partial(_post_kernel, chunk=chunk, final=final)
    def resident(shape):
        return pl.BlockSpec(shape, lambda i: (0, 0), pipeline_mode=pl.Buffered(1))

    return pl.pallas_call(
        kern,
        grid=(m // tm,),
        in_specs=[pl.BlockSpec((tm, d), lambda i: (i, 0)),
                  pl.BlockSpec((tm, d_o), lambda i: (i, 0)),
                  resident((d_o, d)),
                  resident((1, d)),
                  resident((d, d_ff)),
                  resident((d, d_ff)),
                  resident((d_ff, d)),
                  resident((1, d))],
        out_specs=pl.BlockSpec((tm, d), lambda i: (i, 0)),
        out_shape=jax.ShapeDtypeStruct((m, d), F32),
        scratch_shapes=[pltpu.VMEM((tm, d), BF16), pltpu.VMEM((tm, d), F32)],
        compiler_params=_cparams(("arbitrary",)),
        name="post_attn",
    )(x, o, w_o, g, w_gate, w_up, w_down, g_final)


def _diff_lambda(lam_ref, lam_init):
    lp = lam_ref[...]
    s1 = jnp.sum(lp[0:1] * lp[1:2], axis=-1, keepdims=True)
    s2 = jnp.sum(lp[2:3] * lp[3:4], axis=-1, keepdims=True)
    return jnp.exp(s1) - jnp.exp(s2) + lam_init


def _diff_prompt_kernel(q_ref, k_ref, v_ref, bias_ref, lam_ref, subln_ref, o_ref, s_scr, *,
                        lam_init, n_g):
    i = pl.program_id(2)
    prev = jnp.maximum(i - 1, 0)
    n_far = jnp.maximum(i - 1, 0)
    lam = _diff_lambda(lam_ref, lam_init)
    col = lax.broadcasted_iota(jnp.int32, (BLK, 2 * BLK), 1)
    has_prev = (col >= BLK) | (i > 0)
    row_prev = pl.multiple_of(prev * BLK, BLK)
    row_own = pl.multiple_of(i * BLK, BLK)
    dv = v_ref.shape[1]

    for g in range(n_g):
        o_maps = []
        for mp in range(2):
            mi = g * 2 + mp
            q = q_ref[:, mi * HEAD_DIM:(mi + 1) * HEAD_DIM] * 0.125
            ksl = slice(mp * HEAD_DIM, (mp + 1) * HEAD_DIM)

            s_prev = lax.dot_general(q, k_ref[pl.ds(row_prev, BLK), ksl], _NT, preferred_element_type=F32)
            s_own = lax.dot_general(q, k_ref[pl.ds(row_own, BLK), ksl], _NT, preferred_element_type=F32)
            s_near = jnp.concatenate([s_prev, s_own], axis=1) + bias_ref[mi]
            s_near = jnp.where(has_prev, s_near, NEG_INF)
            s_scr[:, 0:2 * BLK] = s_near
            mx0 = jnp.maximum(s_near[:, :BLK], s_near[:, BLK:])

            def far1(j, mx):
                r = pl.multiple_of(j * BLK, BLK)
                s = lax.dot_general(q, k_ref[pl.ds(r, BLK), ksl], _NT, preferred_element_type=F32)
                s_scr[:, pl.ds(pl.multiple_of((j + 2) * BLK, BLK), BLK)] = s
                return jnp.maximum(mx, s)

            mx = lax.fori_loop(0, n_far, far1, mx0)
            m_row = jnp.broadcast_to(jnp.max(mx, axis=1, keepdims=True), (BLK, BLK))

            p0 = jnp.exp(s_scr[:, 0:BLK] - m_row)
            p1 = jnp.exp(s_scr[:, BLK:2 * BLK] - m_row)
            acc0 = (jnp.dot(p0.astype(BF16), v_ref[pl.ds(row_prev, BLK), :], preferred_element_type=F32)
                    + jnp.dot(p1.astype(BF16), v_ref[pl.ds(row_own, BLK), :], preferred_element_type=F32))

            def far2(j, carry):
                lsum, acc = carry
                r = pl.multiple_of(j * BLK, BLK)
                p = jnp.exp(s_scr[:, pl.ds(pl.multiple_of((j + 2) * BLK, BLK), BLK)] - m_row)
                acc = acc + jnp.dot(p.astype(BF16), v_ref[pl.ds(r, BLK), :], preferred_element_type=F32)
                return lsum + p, acc

            lsum, acc = lax.fori_loop(0, n_far, far2, (p0 + p1, acc0))
            o_maps.append(acc / jnp.sum(lsum, axis=1, keepdims=True))

        d = o_maps[0] - lam * o_maps[1]
        d = d * lax.rsqrt(jnp.mean(d * d, axis=-1, keepdims=True) + RMS_EPS) * subln_ref[...]
        d = d * (1.0 - lam_init)
        o_ref[:, g * dv:(g + 1) * dv] = d.astype(o_ref.dtype)


def _diff_attn_prompt(q, kv, bias_causal, lam_p, subln, lam_init, batch, seq, n_kv, n_g):
    dv = 2 * HEAD_DIM
    q3 = q.reshape(batch, seq, q.shape[1])
    kv3 = kv.reshape(batch, seq, kv.shape[1])
    qw = n_g * 2 * HEAD_DIM
    kern = functools.partial(_diff_prompt_kernel, lam_init=lam_init, n_g=n_g)
    o = pl.pallas_call(
        kern,
        grid=(batch, n_kv, seq // BLK),
        in_specs=[pl.BlockSpec((None, BLK, qw), lambda b, k, i: (b, i, k)),
                  pl.BlockSpec((None, seq, dv), lambda b, k, i: (b, 0, k)),
                  pl.BlockSpec((None, seq, dv), lambda b, k, i: (b, 0, n_kv + k)),
                  pl.BlockSpec((n_g * 2, BLK, 2 * BLK), lambda b, k, i: (k, 0, 0)),
                  pl.BlockSpec((4, HEAD_DIM), lambda b, k, i: (0, 0)),
                  pl.BlockSpec((1, dv), lambda b, k, i: (0, 0))],
        out_specs=pl.BlockSpec((None, BLK, n_g * dv), lambda b, k, i: (b, i, k)),
        out_shape=jax.ShapeDtypeStruct((batch, seq, n_kv * n_g * dv), BF16),
        scratch_shapes=[pltpu.VMEM((BLK, seq), F32)],
        compiler_params=_cparams(("arbitrary", "arbitrary", "arbitrary")),
        name="diff_attn_prompt",
    )(q3, kv3, kv3, bias_causal, lam_p, subln)
    return o.reshape(batch * seq, n_kv * n_g * dv)


def _diff_sample_kernel(pt_ref, q_ref, knew_ref, vnew_ref, bias_ref, lam_ref, subln_ref, *rest,
                        pages, lam_init, n_kv):
    k_refs = rest[:pages]
    v_refs = rest[pages:2 * pages]
    o_ref = rest[2 * pages]
    m_scr, l_scr, acc_scr = rest[2 * pages + 1:]
    grp = pl.program_id(1)
    last = pl.num_programs(1) - 1
    n_rows = q_ref.shape[0]
    dv = 2 * HEAD_DIM

    @pl.when(grp == 0)
    def _():
        m_scr[...] = jnp.full(m_scr.shape, NEG_INF, F32)
        l_scr[...] = jnp.zeros(l_scr.shape, F32)
        acc_scr[...] = jnp.zeros(acc_scr.shape, F32)

    qf = q_ref[...].astype(F32) * 0.125
    qb = qf.astype(BF16)
    n_heads = n_rows // 2
    row_kv = (lax.broadcasted_iota(jnp.int32, (n_rows, dv), 0) % n_heads) // (n_heads // n_kv)
    near_bias = jnp.where(grp == last, bias_ref[:, 0:BLK], 0.0)
    s_list = []
    for p in range(pages):
        s = near_bias if p == pages - 1 else jnp.zeros((n_rows, BLK), F32)
        for kk in range(n_kv):
            s = s + lax.dot_general(qb[:, kk * dv:(kk + 1) * dv], k_refs[p][:, kk, :].astype(BF16), _NT,
                                    preferred_element_type=F32)
        s_list.append(s)
    s_all = jnp.concatenate(s_list, axis=1)
    m_old = m_scr[...]
    m_new = jnp.maximum(m_old, jnp.max(s_all, axis=1, keepdims=True))
    alpha = jnp.exp(m_old - m_new)
    p_all = jnp.exp(s_all - m_new)
    l_new = alpha * l_scr[...] + jnp.sum(p_all, axis=1, keepdims=True)
    acc = alpha * acc_scr[...]
    for p in range(pages):
        pp = p_all[:, p * BLK:(p + 1) * BLK]
        for kk in range(n_kv):
            acc = acc + jnp.dot(jnp.where(row_kv == kk, pp, 0.0).astype(BF16),
                                v_refs[p][:, kk, :].astype(BF16), preferred_element_type=F32)
    m_scr[...] = m_new
    l_scr[...] = l_new
    acc_scr[...] = acc

    @pl.when(grp == last)
    def _():
        s_self = jnp.sum(qf * knew_ref[...], axis=1, keepdims=True) + bias_ref[:, BLK:BLK + 1]
        m_f = jnp.maximum(m_new, s_self)
        a = jnp.exp(m_new - m_f)
        p_self = jnp.exp(s_self - m_f)
        l_f = a * l_new + p_self
        v_self = jnp.zeros((n_rows, dv), F32)
        for kk in range(n_kv):
            v_self = v_self + jnp.where(row_kv == kk, vnew_ref[:, kk * dv:(kk + 1) * dv], 0.0)
        o = (a * acc + p_self * v_self) / l_f
        lam = _diff_lambda(lam_ref, lam_init)
        d = o[:n_heads] - lam * o[n_heads:]
        d = d * lax.rsqrt(jnp.mean(d * d, axis=-1, keepdims=True) + RMS_EPS) * subln_ref[...]
        o_ref[...] = (d * (1.0 - lam_init)).astype(o_ref.dtype)


def _diff_attn_sample(q_blk, k_new, v_new, bias_rows, cache_k, cache_v, li, page_table, lam_p, subln,
                      lam_init, n_kv, pages):
    b, n_rows, width = q_blk.shape
    n_pages = page_table.shape[1]
    page, dv = cache_k.shape[2], cache_k.shape[4]
    assert page == BLK and dv == 2 * HEAD_DIM and n_pages % pages == 0
    n_heads = n_rows // 2
    pt_flat = page_table.reshape(-1)

    def page_map(p):
        return lambda s, g, pt: (pt[s * n_pages + g * pages + p], li, 0, 0, 0)

    cache_specs = [pl.BlockSpec((None, None, page, n_kv, dv), page_map(p)) for p in range(pages)]
    kern = functools.partial(_diff_sample_kernel, pages=pages, lam_init=lam_init, n_kv=n_kv)
    grid_spec = pltpu.PrefetchScalarGridSpec(
        num_scalar_prefetch=1,
        grid=(b, n_pages // pages),
        in_specs=[pl.BlockSpec((None, n_rows, width), lambda s, g, pt: (s, 0, 0)),
                  pl.BlockSpec((None, 1, width), lambda s, g, pt: (s, 0, 0)),
                  pl.BlockSpec((None, 1, width), lambda s, g, pt: (s, 0, 0)),
                  pl.BlockSpec((n_rows, 2 * BLK), lambda s, g, pt: (0, 0)),
                  pl.BlockSpec((4, HEAD_DIM), lambda s, g, pt: (0, 0)),
                  pl.BlockSpec((1, dv), lambda s, g, pt: (0, 0))] + cache_specs + cache_specs,
        out_specs=pl.BlockSpec((None, n_heads, dv), lambda s, g, pt: (s, 0, 0)),
        scratch_shapes=[pltpu.VMEM((n_rows, 1), F32), pltpu.VMEM((n_rows, 1), F32),
                        pltpu.VMEM((n_rows, dv), F32)],
    )
    o = pl.pallas_call(
        kern,
        grid_spec=grid_spec,
        out_shape=jax.ShapeDtypeStruct((b, n_heads, dv), BF16),
        compiler_params=_cparams(("arbitrary", "arbitrary")),
        name="diff_attn_sample",
    )(pt_flat, q_blk, k_new.reshape(b, 1, width), v_new.reshape(b, 1, width), bias_rows, lam_p, subln,
      *([cache_k] * pages), *([cache_v] * pages))
    return o.reshape(b, n_heads * dv)


def _swa_prompt_kernel(sink_ref, q_ref, kvp_ref, kvc_ref, bias_ref, o_ref, *, n_kv, n_g):
    i = pl.program_id(1)
    col = lax.broadcasted_iota(jnp.int32, (BLK, 2 * BLK), 1)
    has_prev = (col >= BLK) | (i > 0)
    v_off = n_kv * HEAD_DIM
    outs = []
    for kv in range(n_kv):
        ksl = slice(kv * HEAD_DIM, (kv + 1) * HEAD_DIM)
        vsl = slice(v_off + kv * HEAD_DIM, v_off + (kv + 1) * HEAD_DIM)
        kk = jnp.concatenate([kvp_ref[:, ksl], kvc_ref[:, ksl]], axis=0)
        vv = jnp.concatenate([kvp_ref[:, vsl], kvc_ref[:, vsl]], axis=0)
        for g in range(n_g):
            h = kv * n_g + g
            q = q_ref[:, h * HEAD_DIM:(h + 1) * HEAD_DIM] * 0.125
            s = lax.dot_general(q, kk, _NT, preferred_element_type=F32) + bias_ref[h]
            s = jnp.where(has_prev, s, NEG_INF)
            sink = sink_ref[0, h]
            m = jnp.maximum(jnp.max(s, axis=1, keepdims=True), sink)
            p = jnp.exp(s - m)
            denom = jnp.sum(p, axis=1, keepdims=True) + jnp.exp(sink - m)
            o = jnp.dot(p.astype(BF16), vv, preferred_element_type=F32) / denom
            outs.append(o)
    o_ref[...] = jnp.concatenate(outs, axis=1).astype(o_ref.dtype)


def _swa_attn_prompt(q, kv, bias_band, sinks, batch, seq, n_kv, n_g):
    qw = q.shape[1]
    kvw = kv.shape[1]
    q3 = q.reshape(batch, seq, qw)
    kv3 = kv.reshape(batch, seq, kvw)
    kern = functools.partial(_swa_prompt_kernel, n_kv=n_kv, n_g=n_g)
    o = pl.pallas_call(
        kern,
        grid=(batch, seq // BLK),
        in_specs=[pl.BlockSpec(memory_space=pltpu.SMEM),
                  pl.BlockSpec((None, BLK, qw), lambda b, i: (b, i, 0)),
                  pl.BlockSpec((None, BLK, kvw), lambda b, i: (b, jnp.maximum(i - 1, 0), 0)),
                  pl.BlockSpec((None, BLK, kvw), lambda b, i: (b, i, 0)),
                  pl.BlockSpec((n_kv * n_g, BLK, 2 * BLK), lambda b, i: (0, 0, 0))],
        out_specs=pl.BlockSpec((None, BLK, qw), lambda b, i: (b, i, 0)),
        out_shape=jax.ShapeDtypeStruct((batch, seq, qw), BF16),
        compiler_params=_cparams(("arbitrary", "arbitrary")),
        name="swa_attn_prompt",
    )(sinks.reshape(1, -1), q3, kv3, kv3, bias_band)
    return o.reshape(batch * seq, qw)


def _swa_sample_kernel(q_ref, kvnew_ref, kcol_ref, sk_ref, sv_ref, bias_ref, sink_ref, o_ref, nk_ref, nv_ref,
                       *, n_kv, bb):
    n_heads = q_ref.shape[1]
    n_g = n_heads // n_kv
    buf = sk_ref.shape[-1]
    newest = lax.broadcasted_iota(jnp.int32, (HEAD_DIM, buf), 1) == buf - 1
    for s in range(bb):
        for kv in range(n_kv):
            hs = slice(kv * n_g, (kv + 1) * n_g)
            qf = q_ref[s, hs, :].astype(F32) * 0.125
            k_row = kvnew_ref[s, :, kv * HEAD_DIM:(kv + 1) * HEAD_DIM]
            v_row = kvnew_ref[s, :, (n_kv + kv) * HEAD_DIM:(n_kv + kv + 1) * HEAD_DIM]
            st_k = sk_ref[s, kv]
            st_v = sv_ref[s, kv]
            sink = sink_ref[hs, :]
            sc = jnp.dot(qf.astype(BF16), st_k.astype(BF16), preferred_element_type=F32) + bias_ref[hs, 0:BLK]
            s_self = jnp.sum(qf * k_row, axis=1, keepdims=True) + bias_ref[hs, BLK:BLK + 1]
            m = jnp.maximum(jnp.maximum(jnp.max(sc, axis=1, keepdims=True), s_self), sink)
            p = jnp.exp(sc - m)
            p_self = jnp.exp(s_self - m)
            denom = jnp.sum(p, axis=1, keepdims=True) + p_self + jnp.exp(sink - m)
            o = lax.dot_general(p.astype(BF16), st_v.astype(BF16), _NT, preferred_element_type=F32)
            o_ref[s, hs, :] = ((o + p_self * v_row) / denom).astype(o_ref.dtype)
            nk_ref[s, kv] = jnp.where(newest, kcol_ref[s, kv], pltpu.roll(st_k, buf - 1, axis=1))
            nv_ref[s, kv] = jnp.where(newest, kcol_ref[s, n_kv + kv], pltpu.roll(st_v, buf - 1, axis=1))


def _swa_attn_sample(q, kv_new, state_k_t, state_v_t, li, bias_rows, sinks, bb):
    b, n_heads, _ = q.shape
    n_kv, buf = state_k_t.shape[2], state_k_t.shape[4]
    kw = n_kv * HEAD_DIM
    assert buf == WINDOW and b % bb == 0
    kern = functools.partial(_swa_sample_kernel, n_kv=n_kv, bb=bb)
    st_spec = pl.BlockSpec((bb, None, n_kv, HEAD_DIM, buf), lambda i: (i, li, 0, 0, 0))
    new_spec = pl.BlockSpec((bb, n_kv, HEAD_DIM, buf), lambda i: (i, 0, 0, 0))
    new_shape = jax.ShapeDtypeStruct((b, n_kv, HEAD_DIM, buf), F32)
    o, nk, nv = pl.pallas_call(
        kern,
        grid=(b // bb,),
        in_specs=[pl.BlockSpec((bb, n_heads, HEAD_DIM), lambda i: (i, 0, 0)),
                  pl.BlockSpec((bb, 1, 2 * kw), lambda i: (i, 0, 0)),
                  pl.BlockSpec((bb, 2 * n_kv, HEAD_DIM, 1), lambda i: (i, 0, 0, 0)),
                  st_spec, st_spec,
                  pl.BlockSpec((n_heads, 2 * BLK), lambda i: (0, 0)),
                  pl.BlockSpec((n_heads, 1), lambda i: (0, 0))],
        out_specs=[pl.BlockSpec((bb, n_heads, HEAD_DIM), lambda i: (i, 0, 0)), new_spec, new_spec],
        out_shape=[jax.ShapeDtypeStruct((b, n_heads, HEAD_DIM), F32), new_shape, new_shape],
        compiler_params=_cparams(("arbitrary",)),
        name="swa_attn_sample",
    )(q, kv_new.reshape(b, 1, 2 * kw), kv_new.reshape(b, 2 * n_kv, HEAD_DIM, 1), state_k_t, state_v_t,
      bias_rows, sinks.reshape(-1, 1))
    return o.reshape(b, n_heads * HEAD_DIM).astype(BF16), nk, nv


def _block_rows(q_rows, n_kv_blocks, block_of_row):
    w = q_rows.shape[-1]
    tiled = jnp.tile(q_rows, (1, 1, n_kv_blocks))
    lane_block = np.arange(n_kv_blocks * w)[None, :] // w
    mask = jnp.asarray(lane_block == np.asarray(block_of_row)[:, None])
    return jnp.where(mask[None], tiled, jnp.zeros_like(tiled))


def kernel(x_prompt, x_sample, cache_k_diff, cache_v_diff, state_k_swa, state_v_swa, page_table,
           rel_bias, g_mix, g_ffn, g_final, w_qkv_diff, w_o_diff, lambda_diff, subln_diff,
           w_qkv_swa, b_qkv_swa, w_o_swa, sinks_swa, w_gate, w_up, w_down):
    batch, seq, d_model = x_prompt.shape
    dec_b = x_sample.shape[0]
    depth = g_mix.shape[0]
    n_pool, n_diff_layers, page, n_kv_diff, kd = cache_k_diff.shape
    n_heads_diff = w_o_diff.shape[1] // kd
    g_diff = n_heads_diff // n_kv_diff
    n_kv_swa = state_k_swa.shape[3]
    n_heads_swa = sinks_swa.shape[1]
    g_swa = n_heads_swa // n_kv_swa
    buf = state_k_swa.shape[2]
    q_diff = n_heads_diff * 2 * HEAD_DIM
    q_swa = n_heads_swa * HEAD_DIM
    kw_diff = n_kv_diff * kd
    kw_swa = n_kv_swa * HEAD_DIM

    bias_causal, bias_band = _bias_tiles(rel_bias)
    diff_cols = np.array([2 * (r % n_heads_diff) + r // n_heads_diff for r in range(2 * n_heads_diff)])
    diff_row_kv = np.array([2 * ((r % n_heads_diff) // g_diff) + r // n_heads_diff
                            for r in range(2 * n_heads_diff)])
    bias_rows_diff = bias_causal[:, 0, :][diff_cols]
    bias_rows_swa = bias_band[:, 0, :]

    st_k = state_k_swa.transpose(0, 1, 3, 4, 2)
    st_v = state_v_swa.transpose(0, 1, 3, 4, 2)

    xp = x_prompt.reshape(batch * seq, d_model)
    xs = x_sample.reshape(dec_b, d_model)
    tm_p, tm_s = 512, dec_b
    zero_bias = jnp.zeros((1, w_qkv_diff.shape[2]), F32)
    kdp, vdp, kds, vds, ksp, vsp, kss, vss = [], [], [], [], [], [], [], []

    for i in range(depth):
        li = i // 2
        g_m = g_mix[i].reshape(1, d_model)
        if i % 2 == 0:
            lam_init = 0.8 - 0.6 * math.exp(-0.3 * i)
            w_qkv = w_qkv_diff[li].astype(BF16)
            w_o = w_o_diff[li].astype(BF16)
            subln = subln_diff[li].reshape(1, kd)
            qp, kvp_f, kvp_b = _qkv_proj(xp, g_m, w_qkv, zero_bias, q_diff, tm_p)
            qs, kvs_f, _ = _qkv_proj(xs, g_m, w_qkv, zero_bias, q_diff, tm_s)
            op = _diff_attn_prompt(qp, kvp_b, bias_causal, lambda_diff[li], subln, lam_init,
                                   batch, seq, n_kv_diff, g_diff)
            q_rows = qs.reshape(dec_b, n_kv_diff, g_diff, 2, HEAD_DIM).transpose(0, 3, 1, 2, 4)
            q_blk = _block_rows(q_rows.reshape(dec_b, 2 * n_heads_diff, HEAD_DIM), 2 * n_kv_diff, diff_row_kv)
            os_ = _diff_attn_sample(q_blk, kvs_f[:, :kw_diff], kvs_f[:, kw_diff:], bias_rows_diff,
                                    cache_k_diff, cache_v_diff, li, page_table, lambda_diff[li], subln,
                                    lam_init, n_kv_diff, pages=8)
            kdp.append(kvp_f[:, :kw_diff].reshape(batch, seq, n_kv_diff, kd))
            vdp.append(kvp_f[:, kw_diff:].reshape(batch, seq, n_kv_diff, kd))
            kds.append(kvs_f[:, :kw_diff].reshape(dec_b, 1, n_kv_diff, kd))
            vds.append(kvs_f[:, kw_diff:].reshape(dec_b, 1, n_kv_diff, kd))
        else:
            w_qkv = w_qkv_swa[li].astype(BF16)
            w_o = w_o_swa[li].astype(BF16)
            b_qkv = b_qkv_swa[li].reshape(1, -1)
            qp, kvp_f, kvp_b = _qkv_proj(xp, g_m, w_qkv, b_qkv, q_swa, tm_p)
            qs, kvs_f, _ = _qkv_proj(xs, g_m, w_qkv, b_qkv, q_swa, tm_s)
            op = _swa_attn_prompt(qp, kvp_b, bias_band, sinks_swa[li], batch, seq, n_kv_swa, g_swa)
            os_, nk, nv = _swa_attn_sample(qs.reshape(dec_b, n_heads_swa, HEAD_DIM).astype(F32), kvs_f,
                                           st_k, st_v, li, bias_rows_swa, sinks_swa[li], bb=8)
            tail = kvp_f.reshape(batch, seq, 2 * kw_swa)[:, seq - min(WINDOW, seq):]
            ksp.append(tail[..., :kw_swa].reshape(batch, -1, n_kv_swa, HEAD_DIM))
            vsp.append(tail[..., kw_swa:].reshape(batch, -1, n_kv_swa, HEAD_DIM))
            kss.append(nk.transpose(0, 3, 1, 2))
            vss.append(nv.transpose(0, 3, 1, 2))
        final = i == depth - 1
        g_f = g_ffn[i].reshape(1, d_model)
        gfin = g_final.reshape(1, d_model)
        wg, wu, wd = w_gate[i].astype(BF16), w_up[i].astype(BF16), w_down[i].astype(BF16)
        xp = _post_attn(xp, op, w_o, g_f, wg, wu, wd, gfin, tm_p, final)
        xs = _post_attn(xs, os_, w_o, g_f, wg, wu, wd, gfin, tm_s, final)

    return (xp.reshape(batch, seq, d_model), xs.reshape(dec_b, 1, d_model),
            jnp.stack(kdp, axis=1), jnp.stack(vdp, axis=1), jnp.stack(kds, axis=1), jnp.stack(vds, axis=1),
            jnp.stack(ksp, axis=1), jnp.stack(vsp, axis=1), jnp.stack(kss, axis=1), jnp.stack(vss, axis=1))
```

```python
import functools
import math

import numpy as np
import jax
import jax.numpy as jnp
from jax import lax
from jax.experimental import pallas as pl
from jax.experimental.pallas import tpu as pltpu

F32 = jnp.float32
BF16 = jnp.bfloat16

HEAD_DIM = 64
WINDOW = 128
NUM_BUCKETS = 32
MAX_DISTANCE = 128
RMS_EPS = 1e-6
NEG_INF = -1e30
BLK = 128
TQ = 2 * BLK
LANES = 128
VMEM_LIMIT = 56 * 1024 * 1024

_NT = (((1,), (1,)), ((), ()))


def _rms(x, g):
    return x * lax.rsqrt(jnp.mean(x * x, axis=-1, keepdims=True) + RMS_EPS) * g


def _cparams(sem):
    return pltpu.CompilerParams(dimension_semantics=sem, vmem_limit_bytes=VMEM_LIMIT)


def _bucket_table():
    dist = np.arange(BLK)[:, None] + BLK - np.arange(2 * BLK)[None, :]
    n = np.maximum(dist, 0)
    max_exact = NUM_BUCKETS // 2
    nf = np.maximum(n, 1).astype(np.float32)
    large = max_exact + (np.log(nf / np.float32(max_exact)) / np.float32(math.log(MAX_DISTANCE / max_exact))
                         * np.float32(NUM_BUCKETS - max_exact)).astype(np.int32)
    large = np.minimum(large, NUM_BUCKETS - 1)
    return np.where(n < max_exact, n, large).astype(np.int32)


def _bias_kernel(table_ref, bucket_ref, causal_ref, band_ref):
    col = pl.program_id(0)
    bucket = bucket_ref[...]
    t = jnp.zeros(bucket.shape, F32)
    for b in range(NUM_BUCKETS):
        t = jnp.where(bucket == b, table_ref[b, col], t)
    far = table_ref[NUM_BUCKETS - 1, col]
    dist = (lax.broadcasted_iota(jnp.int32, bucket.shape, 0) + BLK
            - lax.broadcasted_iota(jnp.int32, bucket.shape, 1))
    causal_ref[...] = jnp.where(dist >= 0, t - far, NEG_INF)
    band_ref[...] = jnp.where((dist >= 0) & (dist < WINDOW), t, NEG_INF)


def _bias_tiles(rel_bias):
    n_cols = rel_bias.shape[1]
    bucket = jnp.asarray(_bucket_table())
    shape = jax.ShapeDtypeStruct((n_cols, BLK, 2 * BLK), F32)
    return pl.pallas_call(
        _bias_kernel,
        grid=(n_cols,),
        in_specs=[pl.BlockSpec(memory_space=pltpu.SMEM),
                  pl.BlockSpec((BLK, 2 * BLK), lambda c: (0, 0))],
        out_specs=[pl.BlockSpec((None, BLK, 2 * BLK), lambda c: (c, 0, 0)),
                   pl.BlockSpec((None, BLK, 2 * BLK), lambda c: (c, 0, 0))],
        out_shape=[shape, shape],
        compiler_params=_cparams(("arbitrary",)),
        name="bias_tiles",
    )(rel_bias, bucket)


def _qkv_kernel(x_ref, g_ref, w_ref, b_ref, q_ref, kvf_ref, kvb_ref, h_scr, *, n_q, chunk):
    h_scr[...] = _rms(x_ref[...], g_ref[...]).astype(BF16)
    n_total = w_ref.shape[1]
    for c0 in range(0, n_total, chunk):
        c1 = min(c0 + chunk, n_total)
        acc = jnp.dot(h_scr[...], w_ref[:, c0:c1], preferred_element_type=F32) + b_ref[:, c0:c1]
        if c1 <= n_q:
            q_ref[:, c0:c1] = acc.astype(BF16)
        else:
            kvf_ref[:, c0 - n_q:c1 - n_q] = acc
            kvb_ref[:, c0 - n_q:c1 - n_q] = acc.astype(BF16)


def _qkv_proj(x, g, w, b, n_q, tm):
    m, d = x.shape
    n = w.shape[1]
    n_kv = n - n_q
    chunk = 256
    assert m % tm == 0 and n_q % chunk == 0 and n_kv % chunk == 0
    kern = functools.partial(_qkv_kernel, n_q=n_q, chunk=chunk)
    return pl.pallas_call(
        kern,
        grid=(m // tm,),
        in_specs=[pl.BlockSpec((tm, d), lambda i: (i, 0)),
                  pl.BlockSpec((1, d), lambda i: (0, 0)),
                  pl.BlockSpec((d, n), lambda i: (0, 0)),
                  pl.BlockSpec((1, n), lambda i: (0, 0))],
        out_specs=[pl.BlockSpec((tm, n_q), lambda i: (i, 0)),
                   pl.BlockSpec((tm, n_kv), lambda i: (i, 0)),
                   pl.BlockSpec((tm, n_kv), lambda i: (i, 0))],
        out_shape=[jax.ShapeDtypeStruct((m, n_q), BF16),
                   jax.ShapeDtypeStruct((m, n_kv), F32),
                   jax.ShapeDtypeStruct((m, n_kv), BF16)],
        scratch_shapes=[pltpu.VMEM((tm, d), BF16)],
        compiler_params=_cparams(("arbitrary",)),
        name="qkv_proj",
    )(x, g, w, b)


def _qkv_paged_kernel(x_ref, g_ref, w_ref, *rest, n_q, n_kv, chunk, first):
    q_ref, kvb_ref, ko_ref, vo_ref, h_scr = rest if first else rest[2:]
    tm = x_ref.shape[0]
    dv = ko_ref.shape[-1]
    h_scr[...] = _rms(x_ref[...], g_ref[...]).astype(BF16)
    n_total = w_ref.shape[1]
    for c0 in range(0, n_total, chunk):
        acc = jnp.dot(h_scr[...], w_ref[:, c0:c0 + chunk], preferred_element_type=F32)
        if c0 < n_q:
            q_ref[:, c0:c0 + chunk] = acc.astype(BF16)
            continue
        kvb_ref[:, c0 - n_q:c0 - n_q + chunk] = acc.astype(BF16)
        for j in range(chunk // dv):
            head = (c0 - n_q) // dv + j
            dst = ko_ref if head < n_kv else vo_ref
            rows = pl.ds(head % n_kv, tm, stride=n_kv)
            val = acc[:, j * dv:(j + 1) * dv]
            if first:
                for layer in range(dst.shape[0]):
                    dst[layer, rows, :] = val
            else:
                dst[rows, :] = val


def _qkv_proj_paged(x, g, w, n_q, n_kv, tm, batch, layer, n_layers, prev):
    m, d = x.shape
    n = w.shape[1]
    seq = m // batch
    dv = (n - n_q) // (2 * n_kv)
    chunk = 256
    first = prev is None
    assert seq % tm == 0 and n_q % chunk == 0 and (n - n_q) % chunk == 0 and chunk % dv == 0
    tiles = seq // tm
    kern = functools.partial(_qkv_paged_kernel, n_q=n_q, n_kv=n_kv, chunk=chunk, first=first)
    cache_shape = jax.ShapeDtypeStruct((batch, n_layers, seq * n_kv, dv), F32)
    if first:
        cache_spec = pl.BlockSpec((None, n_layers, tm * n_kv, dv), lambda i: (i // tiles, 0, i % tiles, 0))
        extra_in, extra_args, aliases = [], [], {}
    else:
        cache_spec = pl.BlockSpec((None, None, tm * n_kv, dv), lambda i: (i // tiles, layer, i % tiles, 0))
        extra_in = [pl.BlockSpec(memory_space=pl.ANY)] * 2
        extra_args, aliases = list(prev), {3: 2, 4: 3}
    return pl.pallas_call(
        kern,
        grid=(m // tm,),
        in_specs=[pl.BlockSpec((tm, d), lambda i: (i, 0)),
                  pl.BlockSpec((1, d), lambda i: (0, 0)),
                  pl.BlockSpec((d, n), lambda i: (0, 0))] + extra_in,
        out_specs=[pl.BlockSpec((tm, n_q), lambda i: (i, 0)),
                   pl.BlockSpec((tm, n - n_q), lambda i: (i, 0)),
                   cache_spec, cache_spec],
        out_shape=[jax.ShapeDtypeStruct((m, n_q), BF16),
                   jax.ShapeDtypeStruct((m, n - n_q), BF16),
                   cache_shape, cache_shape],
        scratch_shapes=[pltpu.VMEM((tm, d), BF16)],
        input_output_aliases=aliases,
        compiler_params=_cparams(("arbitrary",)),
        name="qkv_proj_paged",
    )(x, g, w, *extra_args)


def _post_kernel(x_ref, o_ref, wo_ref, g_ref, wg_ref, wu_ref, wd_ref, gf_ref, out_ref,
                 h_scr, acc_scr, *, chunk, final):
    x1 = x_ref[...] + jnp.dot(o_ref[...], wo_ref[...], preferred_element_type=F32)
    acc_scr[...] = x1
    h_scr[...] = _rms(x1, g_ref[...]).astype(BF16)
    d_ff = wg_ref.shape[1]
    for c0 in range(0, d_ff, chunk):
        gate = jnp.dot(h_scr[...], wg_ref[:, c0:c0 + chunk], preferred_element_type=F32)
        up = jnp.dot(h_scr[...], wu_ref[:, c0:c0 + chunk], preferred_element_type=F32)
        act = (gate * (1.0 / (1.0 + jnp.exp(-gate))) * up).astype(BF16)
        acc_scr[...] += jnp.dot(act, wd_ref[c0:c0 + chunk, :], preferred_element_type=F32)
    if final:
        out_ref[...] = _rms(acc_scr[...], gf_ref[...])
    else:
        out_ref[...] = acc_scr[...]


def _post_attn(x, o, w_o, g, w_gate, w_up, w_down, g_final, tm, final):
    m, d = x.shape
    d_o = o.shape[1]
    d_ff = w_gate.shape[1]
    chunk = 256
    assert m % tm == 0 and d_ff % chunk == 0
    kern = functools.partial(_post_kernel, chunk=chunk, final=final)
    def resident(shape):
        return pl.BlockSpec(shape, lambda i: (0, 0), pipeline_mode=pl.Buffered(1))

    return pl.pallas_call(
        kern,
        grid=(m // tm,),
        in_specs=[pl.BlockSpec((tm, d), lambda i: (i, 0)),
                  pl.BlockSpec((tm, d_o), lambda i: (i, 0)),
                  resident((d_o, d)),
                  resident((1, d)),
                  resident((d, d_ff)),
                  resident((d, d_ff)),
                  resident((d_ff, d)),
                  resident((1, d))],
        out_specs=pl.BlockSpec((tm, d), lambda i: (i, 0)),
        out_shape=jax.ShapeDtypeStruct((m, d), F32),
        scratch_shapes=[pltpu.VMEM((tm, d), BF16), pltpu.VMEM((tm, d), F32)],
        compiler_params=_cparams(("arbitrary",)),
        name="post_attn",
    )(x, o, w_o, g, w_gate, w_up, w_down, g_final)


def _diff_lambda(lam_ref, lam_init):
    lp = lam_ref[...]
    s1 = jnp.sum(lp[0:1] * lp[1:2], axis=-1, keepdims=True)
    s2 = jnp.sum(lp[2:3] * lp[3:4], axis=-1, keepdims=True)
    return jnp.exp(s1) - jnp.exp(s2) + lam_init


def _near_bias(s, bias_ref, mp, n_g, diag):
    parts = []
    for g in range(n_g):
        tile = bias_ref[g * 2 + mp]
        t_prev, t_own = tile[:, :BLK], tile[:, BLK:]
        top = s[g * TQ:g * TQ + BLK]
        bot = s[g * TQ + BLK:(g + 1) * TQ]
        if diag:
            top = jnp.concatenate([top[:, :BLK] + t_own, jnp.full((BLK, BLK), NEG_INF, F32)], axis=1)
            bot = jnp.concatenate([bot[:, :BLK] + t_prev, bot[:, BLK:] + t_own], axis=1)
        else:
            top = jnp.concatenate([top[:, :BLK], top[:, BLK:] + t_prev], axis=1)
        parts += [top, bot]
    return jnp.concatenate(parts, axis=0)


def _diff_prompt_block(n_chunks, q_ref, k_ref, v_ref, bias_ref, lam_ref, subln_ref, o_ref, s_scr, m_scr, *,
                       lam_init, n_g):
    dv = v_ref.shape[1]
    lam = _diff_lambda(lam_ref, lam_init)
    o_maps = []
    for mp in range(2):
        q = jnp.concatenate([q_ref[:, (g * 2 + mp) * HEAD_DIM:(g * 2 + mp + 1) * HEAD_DIM]
                             for g in range(n_g)], axis=0) * 0.125
        ksl = slice(mp * HEAD_DIM, (mp + 1) * HEAD_DIM)

        mx = None
        for c in range(n_chunks):
            s = lax.dot_general(q, k_ref[c * TQ:(c + 1) * TQ, ksl], _NT, preferred_element_type=F32)
            if c >= n_chunks - 2:
                s = _near_bias(s, bias_ref, mp, n_g, diag=(c == n_chunks - 1))
            s_scr[:, c * TQ:(c + 1) * TQ] = s
            cm = jnp.maximum(s[:, :BLK], s[:, BLK:])
            mx = cm if mx is None else jnp.maximum(mx, cm)
        m_scr[...] = jnp.broadcast_to(jnp.max(mx, axis=1, keepdims=True), m_scr.shape)

        lsum = None
        acc = None
        for c in range(n_chunks):
            p0 = jnp.exp(s_scr[:, c * TQ:c * TQ + BLK] - m_scr[...])
            p1 = jnp.exp(s_scr[:, c * TQ + BLK:(c + 1) * TQ] - m_scr[...])
            pv = jnp.dot(jnp.concatenate([p0, p1], axis=1).astype(BF16), v_ref[c * TQ:(c + 1) * TQ, :],
                         preferred_element_type=F32)
            lsum = p0 + p1 if lsum is None else lsum + p0 + p1
            acc = pv if acc is None else acc + pv
        o_maps.append(acc / jnp.sum(lsum, axis=1, keepdims=True))

    d = o_maps[0] - lam * o_maps[1]
    d = d * lax.rsqrt(jnp.mean(d * d, axis=-1, keepdims=True) + RMS_EPS) * subln_ref[...]
    d = d * (1.0 - lam_init)
    for g in range(n_g):
        o_ref[:, g * dv:(g + 1) * dv] = d[g * TQ:(g + 1) * TQ].astype(o_ref.dtype)


def _diff_prompt_kernel(q_ref, k_ref, v_ref, bias_ref, lam_ref, subln_ref, o_ref, s_scr, m_scr, *,
                        lam_init, n_g, n_qblk):
    qi = pl.program_id(2)
    for blk in range(n_qblk):
        pl.when(qi == blk)(functools.partial(
            _diff_prompt_block, blk + 1, q_ref, k_ref, v_ref, bias_ref, lam_ref, subln_ref, o_ref, s_scr, m_scr,
            lam_init=lam_init, n_g=n_g))


def _diff_attn_prompt(q, kv, bias_causal, lam_p, subln, lam_init, batch, seq, n_kv, n_g):
    dv = 2 * HEAD_DIM
    q3 = q.reshape(batch, seq, q.shape[1])
    kv3 = kv.reshape(batch, seq, kv.shape[1])
    qw = n_g * 2 * HEAD_DIM
    n_qblk = seq // TQ
    assert seq % TQ == 0
    kern = functools.partial(_diff_prompt_kernel, lam_init=lam_init, n_g=n_g, n_qblk=n_qblk)
    o = pl.pallas_call(
        kern,
        grid=(batch, n_kv, n_qblk),
        in_specs=[pl.BlockSpec((None, TQ, qw), lambda b, k, i: (b, i, k)),
                  pl.BlockSpec((None, seq, dv), lambda b, k, i: (b, 0, k)),
                  pl.BlockSpec((None, seq, dv), lambda b, k, i: (b, 0, n_kv + k)),
                  pl.BlockSpec((n_g * 2, BLK, 2 * BLK), lambda b, k, i: (k, 0, 0)),
                  pl.BlockSpec((4, HEAD_DIM), lambda b, k, i: (0, 0)),
                  pl.BlockSpec((1, dv), lambda b, k, i: (0, 0))],
        out_specs=pl.BlockSpec((None, TQ, n_g * dv), lambda b, k, i: (b, i, k)),
        out_shape=jax.ShapeDtypeStruct((batch, seq, n_kv * n_g * dv), BF16),
        scratch_shapes=[pltpu.VMEM((n_g * TQ, seq), F32), pltpu.VMEM((n_g * TQ, BLK), F32)],
        compiler_params=_cparams(("arbitrary", "arbitrary", "arbitrary")),
        name="diff_attn_prompt",
    )(q3, kv3, kv3, bias_causal, lam_p, subln)
    return o.reshape(batch * seq, n_kv * n_g * dv)


def _diff_sample_kernel(pt_ref, q_ref, knew_ref, vnew_ref, bias_ref, lam_ref, subln_ref, *rest,
                        pages, lam_init, n_kv):
    k_refs = rest[:pages]
    v_refs = rest[pages:2 * pages]
    o_ref = rest[2 * pages]
    m_scr, l_scr, acc_scr = rest[2 * pages + 1:]
    grp = pl.program_id(1)
    last = pl.num_programs(1) - 1
    n_rows, dv = q_ref.shape
    n_heads = n_rows // 2
    prows = k_refs[0].shape[0]

    @pl.when(grp == 0)
    def _():
        m_scr[...] = jnp.full(m_scr.shape, NEG_INF, F32)
        l_scr[...] = jnp.zeros(l_scr.shape, F32)
        acc_scr[...] = jnp.zeros(acc_scr.shape, F32)

    qf = q_ref[...].astype(F32) * 0.125
    qb = qf.astype(BF16)

    def head_of_row(shape):
        return (lax.broadcasted_iota(jnp.int32, shape, 0) % n_heads) // (n_heads // n_kv)

    own = head_of_row((n_rows, prows)) == lax.broadcasted_iota(jnp.int32, (n_rows, prows), 1) % n_kv
    near_bias = jnp.where(grp == last, bias_ref[:, 0:prows], 0.0)
    s_list = []
    for p in range(pages):
        s = lax.dot_general(qb, k_refs[p][...].astype(BF16), _NT, preferred_element_type=F32)
        if p == pages - 1:
            s = s + near_bias
        s_list.append(jnp.where(own, s, NEG_INF))
    s_all = jnp.concatenate(s_list, axis=1)
    m_old = m_scr[...]
    m_new = jnp.maximum(m_old, jnp.max(s_all, axis=1, keepdims=True))
    alpha = jnp.exp(m_old - m_new)
    p_all = jnp.exp(s_all - m_new)
    l_new = alpha * l_scr[...] + jnp.sum(p_all, axis=1, keepdims=True)
    acc = alpha * acc_scr[...]
    for p in range(pages):
        acc = acc + jnp.dot(p_all[:, p * prows:(p + 1) * prows].astype(BF16), v_refs[p][...].astype(BF16),
                            preferred_element_type=F32)
    m_scr[...] = m_new
    l_scr[...] = l_new
    acc_scr[...] = acc

    @pl.when(grp == last)
    def _():
        row_kv = head_of_row((n_rows, dv))
        k_self = jnp.zeros((n_rows, dv), F32)
        v_self = jnp.zeros((n_rows, dv), F32)
        for kk in range(n_kv):
            k_self = k_self + jnp.where(row_kv == kk, knew_ref[:, kk * dv:(kk + 1) * dv], 0.0)
            v_self = v_self + jnp.where(row_kv == kk, vnew_ref[:, kk * dv:(kk + 1) * dv], 0.0)
        s_self = jnp.sum(qf * k_self, axis=1, keepdims=True) + bias_ref[:, prows:prows + 1]
        m_f = jnp.maximum(m_new, s_self)
        a = jnp.exp(m_new - m_f)
        p_self = jnp.exp(s_self - m_f)
        l_f = a * l_new + p_self
        o = (a * acc + p_self * v_self) / l_f
        lam = _diff_lambda(lam_ref, lam_init)
        d = o[:n_heads] - lam * o[n_heads:]
        d = d * lax.rsqrt(jnp.mean(d * d, axis=-1, keepdims=True) + RMS_EPS) * subln_ref[...]
        o_ref[...] = (d * (1.0 - lam_init)).astype(o_ref.dtype)


def _diff_attn_sample(q_rows, k_new, v_new, bias_rows, cache_k, cache_v, li, page_table, lam_p, subln,
                      lam_init, pages):
    b, n_rows, dv = q_rows.shape
    n_pool, n_layers, page, n_kv, _ = cache_k.shape
    n_pages = page_table.shape[1]
    assert n_pages % pages == 0
    n_heads = n_rows // 2
    width = n_kv * dv
    prows = page * n_kv
    cache_k = cache_k.reshape(n_pool, n_layers, prows, dv)
    cache_v = cache_v.reshape(n_pool, n_layers, prows, dv)
    pt_flat = page_table.reshape(-1)

    def page_map(p):
        return lambda s, g, pt: (pt[s * n_pages + g * pages + p], li, 0, 0)

    cache_specs = [pl.BlockSpec((None, None, prows, dv), page_map(p)) for p in range(pages)]
    kern = functools.partial(_diff_sample_kernel, pages=pages, lam_init=lam_init, n_kv=n_kv)
    grid_spec = pltpu.PrefetchScalarGridSpec(
        num_scalar_prefetch=1,
        grid=(b, n_pages // pages),
        in_specs=[pl.BlockSpec((None, n_rows, dv), lambda s, g, pt: (s, 0, 0)),
                  pl.BlockSpec((None, 1, width), lambda s, g, pt: (s, 0, 0)),
                  pl.BlockSpec((None, 1, width), lambda s, g, pt: (s, 0, 0)),
                  pl.BlockSpec(bias_rows.shape, lambda s, g, pt: (0, 0)),
                  pl.BlockSpec((4, HEAD_DIM), lambda s, g, pt: (0, 0)),
                  pl.BlockSpec((1, dv), lambda s, g, pt: (0, 0))] + cache_specs + cache_specs,
        out_specs=pl.BlockSpec((None, n_heads, dv), lambda s, g, pt: (s, 0, 0)),
        scratch_shapes=[pltpu.VMEM((n_rows, 1), F32), pltpu.VMEM((n_rows, 1), F32),
                        pltpu.VMEM((n_rows, dv), F32)],
    )
    o = pl.pallas_call(
        kern,
        grid_spec=grid_spec,
        out_shape=jax.ShapeDtypeStruct((b, n_heads, dv), BF16),
        compiler_params=_cparams(("arbitrary", "arbitrary")),
        name="diff_attn_sample",
    )(pt_flat, q_rows, k_new.reshape(b, 1, width), v_new.reshape(b, 1, width), bias_rows, lam_p, subln,
      *([cache_k] * pages), *([cache_v] * pages))
    return o.reshape(b, n_heads * dv)


def _swa_prompt_kernel(sink_ref, q_ref, kvp_ref, kvc_ref, bias_ref, o_ref, *, n_kv, n_g):
    i = pl.program_id(1)
    col = lax.broadcasted_iota(jnp.int32, (BLK, 2 * BLK), 1)
    has_prev = (col >= BLK) | (i > 0)
    v_off = n_kv * HEAD_DIM
    outs = []
    for kv in range(n_kv):
        ksl = slice(kv * HEAD_DIM, (kv + 1) * HEAD_DIM)
        vsl = slice(v_off + kv * HEAD_DIM, v_off + (kv + 1) * HEAD_DIM)
        kk = jnp.concatenate([kvp_ref[:, ksl], kvc_ref[:, ksl]], axis=0)
        vv = jnp.concatenate([kvp_ref[:, vsl], kvc_ref[:, vsl]], axis=0)
        for g in range(n_g):
            h = kv * n_g + g
            q = q_ref[:, h * HEAD_DIM:(h + 1) * HEAD_DIM] * 0.125
            s = lax.dot_general(q, kk, _NT, preferred_element_type=F32) + bias_ref[h]
            s = jnp.where(has_prev, s, NEG_INF)
            sink = sink_ref[0, h]
            m = jnp.maximum(jnp.max(s, axis=1, keepdims=True), sink)
            p = jnp.exp(s - m)
            denom = jnp.sum(p, axis=1, keepdims=True) + jnp.exp(sink - m)
            o = jnp.dot(p.astype(BF16), vv, preferred_element_type=F32) / denom
            outs.append(o)
    o_ref[...] = jnp.concatenate(outs, axis=1).astype(o_ref.dtype)


def _swa_attn_prompt(q, kv, bias_band, sinks, batch, seq, n_kv, n_g):
    qw = q.shape[1]
    kvw = kv.shape[1]
    q3 = q.reshape(batch, seq, qw)
    kv3 = kv.reshape(batch, seq, kvw)
    kern = functools.partial(_swa_prompt_kernel, n_kv=n_kv, n_g=n_g)
    o = pl.pallas_call(
        kern,
        grid=(batch, seq // BLK),
        in_specs=[pl.BlockSpec(memory_space=pltpu.SMEM),
                  pl.BlockSpec((None, BLK, qw), lambda b, i: (b, i, 0)),
                  pl.BlockSpec((None, BLK, kvw), lambda b, i: (b, jnp.maximum(i - 1, 0), 0)),
                  pl.BlockSpec((None, BLK, kvw), lambda b, i: (b, i, 0)),
                  pl.BlockSpec((n_kv * n_g, BLK, 2 * BLK), lambda b, i: (0, 0, 0))],
        out_specs=pl.BlockSpec((None, BLK, qw), lambda b, i: (b, i, 0)),
        out_shape=jax.ShapeDtypeStruct((batch, seq, qw), BF16),
        compiler_params=_cparams(("arbitrary", "arbitrary")),
        name="swa_attn_prompt",
    )(sinks.reshape(1, -1), q3, kv3, kv3, bias_band)
    return o.reshape(batch * seq, qw)


def _swa_sample_kernel(q_ref, kvnew_ref, kcol_ref, sk_ref, sv_ref, bias_ref, sink_ref, o_ref, nk_ref, nv_ref,
                       *, n_kv, bb):
    n_heads = q_ref.shape[1]
    n_g = n_heads // n_kv
    buf = sk_ref.shape[-1]
    newest = lax.broadcasted_iota(jnp.int32, (HEAD_DIM, buf), 1) == buf - 1
    for s in range(bb):
        for kv in range(n_kv):
            hs = slice(kv * n_g, (kv + 1) * n_g)
            qf = q_ref[s, hs, :].astype(F32) * 0.125
            k_row = kvnew_ref[s, :, kv * HEAD_DIM:(kv + 1) * HEAD_DIM]
            v_row = kvnew_ref[s, :, (n_kv + kv) * HEAD_DIM:(n_kv + kv + 1) * HEAD_DIM]
            st_k = sk_ref[s, kv]
            st_v = sv_ref[s, kv]
            sink = sink_ref[hs, :]
            sc = jnp.dot(qf.astype(BF16), st_k.astype(BF16), preferred_element_type=F32) + bias_ref[hs, 0:BLK]
            s_self = jnp.sum(qf * k_row, axis=1, keepdims=True) + bias_ref[hs, BLK:BLK + 1]
            m = jnp.maximum(jnp.maximum(jnp.max(sc, axis=1, keepdims=True), s_self), sink)
            p = jnp.exp(sc - m)
            p_self = jnp.exp(s_self - m)
            denom = jnp.sum(p, axis=1, keepdims=True) + p_self + jnp.exp(sink - m)
            o = lax.dot_general(p.astype(BF16), st_v.astype(BF16), _NT, preferred_element_type=F32)
            o_ref[s, hs, :] = ((o + p_self * v_row) / denom).astype(o_ref.dtype)
            nk_ref[s, kv] = jnp.where(newest, kcol_ref[s, kv], pltpu.roll(st_k, buf - 1, axis=1))
            nv_ref[s, kv] = jnp.where(newest, kcol_ref[s, n_kv + kv], pltpu.roll(st_v, buf - 1, axis=1))


def _swa_attn_sample(q, kv_new, state_k_t, state_v_t, li, bias_rows, sinks, bb):
    b, n_heads, _ = q.shape
    n_kv, buf = state_k_t.shape[2], state_k_t.shape[4]
    kw = n_kv * HEAD_DIM
    assert buf == WINDOW and b % bb == 0
    kern = functools.partial(_swa_sample_kernel, n_kv=n_kv, bb=bb)
    st_spec = pl.BlockSpec((bb, None, n_kv, HEAD_DIM, buf), lambda i: (i, li, 0, 0, 0))
    new_spec = pl.BlockSpec((bb, n_kv, HEAD_DIM, buf), lambda i: (i, 0, 0, 0))
    new_shape = jax.ShapeDtypeStruct((b, n_kv, HEAD_DIM, buf), F32)
    o, nk, nv = pl.pallas_call(
        kern,
        grid=(b // bb,),
        in_specs=[pl.BlockSpec((bb, n_heads, HEAD_DIM), lambda i: (i, 0, 0)),
                  pl.BlockSpec((bb, 1, 2 * kw), lambda i: (i, 0, 0)),
                  pl.BlockSpec((bb, 2 * n_kv, HEAD_DIM, 1), lambda i: (i, 0, 0, 0)),
                  st_spec, st_spec,
                  pl.BlockSpec((n_heads, 2 * BLK), lambda i: (0, 0)),
                  pl.BlockSpec((n_heads, 1), lambda i: (0, 0))],
        out_specs=[pl.BlockSpec((bb, n_heads, HEAD_DIM), lambda i: (i, 0, 0)), new_spec, new_spec],
        out_shape=[jax.ShapeDtypeStruct((b, n_heads, HEAD_DIM), F32), new_shape, new_shape],
        compiler_params=_cparams(("arbitrary",)),
        name="swa_attn_sample",
    )(q, kv_new.reshape(b, 1, 2 * kw), kv_new.reshape(b, 2 * n_kv, HEAD_DIM, 1), state_k_t, state_v_t,
      bias_rows, sinks.reshape(-1, 1))
    return o.reshape(b, n_heads * HEAD_DIM).astype(BF16), nk, nv


def _block_rows(q_rows, n_kv_blocks, block_of_row):
    w = q_rows.shape[-1]
    tiled = jnp.tile(q_rows, (1, 1, n_kv_blocks))
    lane_block = np.arange(n_kv_blocks * w)[None, :] // w
    mask = jnp.asarray(lane_block == np.asarray(block_of_row)[:, None])
    return jnp.where(mask[None], tiled, jnp.zeros_like(tiled))


def kernel(x_prompt, x_sample, cache_k_diff, cache_v_diff, state_k_swa, state_v_swa, page_table,
           rel_bias, g_mix, g_ffn, g_final, w_qkv_diff, w_o_diff, lambda_diff, subln_diff,
           w_qkv_swa, b_qkv_swa, w_o_swa, sinks_swa, w_gate, w_up, w_down):
    batch, seq, d_model = x_prompt.shape
    dec_b = x_sample.shape[0]
    depth = g_mix.shape[0]
    n_pool, n_diff_layers, page, n_kv_diff, kd = cache_k_diff.shape
    n_heads_diff = w_o_diff.shape[1] // kd
    g_diff = n_heads_diff // n_kv_diff
    n_kv_swa = state_k_swa.shape[3]
    n_heads_swa = sinks_swa.shape[1]
    g_swa = n_heads_swa // n_kv_swa
    buf = state_k_swa.shape[2]
    q_diff = n_heads_diff * 2 * HEAD_DIM
    q_swa = n_heads_swa * HEAD_DIM
    kw_diff = n_kv_diff * kd
    kw_swa = n_kv_swa * HEAD_DIM

    bias_causal, bias_band = _bias_tiles(rel_bias)
    diff_cols = np.array([2 * (r % n_heads_diff) + r // n_heads_diff for r in range(2 * n_heads_diff)])
    diff_row_map = np.arange(2 * n_heads_diff) // n_heads_diff
    near = bias_causal[:, 0, :][diff_cols]
    bias_rows_diff = jnp.concatenate([jnp.repeat(near[:, :BLK], n_kv_diff, axis=1), near[:, BLK:]], axis=1)
    bias_rows_swa = bias_band[:, 0, :]

    st_k = state_k_swa.transpose(0, 1, 3, 4, 2)
    st_v = state_v_swa.transpose(0, 1, 3, 4, 2)

    xp = x_prompt.reshape(batch * seq, d_model)
    xs = x_sample.reshape(dec_b, d_model)
    tm_p, tm_s = 512, dec_b
    zero_bias = jnp.zeros((1, w_qkv_diff.shape[2]), F32)
    kds, vds, ksp, vsp, kss, vss = [], [], [], [], [], []
    new_cache = None

    for i in range(depth):
        li = i // 2
        g_m = g_mix[i].reshape(1, d_model)
        if i % 2 == 0:
            lam_init = 0.8 - 0.6 * math.exp(-0.3 * i)
            w_qkv = w_qkv_diff[li].astype(BF16)
            w_o = w_o_diff[li].astype(BF16)
            subln = subln_diff[li].reshape(1, kd)
            qp, kvp_b, k_cache, v_cache = _qkv_proj_paged(xp, g_m, w_qkv, q_diff, n_kv_diff, tm_p, batch, li,
                                                          n_diff_layers, new_cache)
            new_cache = (k_cache, v_cache)
            qs, kvs_f, _ = _qkv_proj(xs, g_m, w_qkv, zero_bias, q_diff, tm_s)
            op = _diff_attn_prompt(qp, kvp_b, bias_causal, lambda_diff[li], subln, lam_init,
                                   batch, seq, n_kv_diff, g_diff)
            q_rows = qs.reshape(dec_b, n_kv_diff, g_diff, 2, HEAD_DIM).transpose(0, 3, 1, 2, 4)
            q_rows = _block_rows(q_rows.reshape(dec_b, 2 * n_heads_diff, HEAD_DIM), 2, diff_row_map)
            os_ = _diff_attn_sample(q_rows, kvs_f[:, :kw_diff], kvs_f[:, kw_diff:], bias_rows_diff,
                                    cache_k_diff, cache_v_diff, li, page_table, lambda_diff[li], subln,
                                    lam_init, pages=16)
            kds.append(kvs_f[:, :kw_diff].reshape(dec_b, 1, n_kv_diff, kd))
            vds.append(kvs_f[:, kw_diff:].reshape(dec_b, 1, n_kv_diff, kd))
        else:
            w_qkv = w_qkv_swa[li].astype(BF16)
            w_o = w_o_swa[li].astype(BF16)
            b_qkv = b_qkv_swa[li].reshape(1, -1)
            qp, kvp_f, kvp_b = _qkv_proj(xp, g_m, w_qkv, b_qkv, q_swa, tm_p)
            qs, kvs_f, _ = _qkv_proj(xs, g_m, w_qkv, b_qkv, q_swa, tm_s)
            op = _swa_attn_prompt(qp, kvp_b, bias_band, sinks_swa[li], batch, seq, n_kv_swa, g_swa)
            os_, nk, nv = _swa_attn_sample(qs.reshape(dec_b, n_heads_swa, HEAD_DIM).astype(F32), kvs_f,
                                           st_k, st_v, li, bias_rows_swa, sinks_swa[li], bb=8)
            tail = kvp_f.reshape(batch, seq, 2 * kw_swa)[:, seq - min(WINDOW, seq):]
            ksp.append(tail[..., :kw_swa].reshape(batch, -1, n_kv_swa, HEAD_DIM))
            vsp.append(tail[..., kw_swa:].reshape(batch, -1, n_kv_swa, HEAD_DIM))
            kss.append(nk.transpose(0, 3, 1, 2))
            vss.append(nv.transpose(0, 3, 1, 2))
        final = i == depth - 1
        g_f = g_ffn[i].reshape(1, d_model)
        gfin = g_final.reshape(1, d_model)
        wg, wu, wd = w_gate[i].astype(BF16), w_up[i].astype(BF16), w_down[i].astype(BF16)
        xp = _post_attn(xp, op, w_o, g_f, wg, wu, wd, gfin, tm_p, final)
        xs = _post_attn(xs, os_, w_o, g_f, wg, wu, wd, gfin, tm_s, final)

    k_cache, v_cache = new_cache
    cache_shape = (batch, n_diff_layers, seq, n_kv_diff, kd)
    return (xp.reshape(batch, seq, d_model), xs.reshape(dec_b, 1, d_model),
            k_cache.reshape(cache_shape), v_cache.reshape(cache_shape),
            jnp.stack(kds, axis=1), jnp.stack(vds, axis=1),
            jnp.stack(ksp, axis=1), jnp.stack(vsp, axis=1), jnp.stack(kss, axis=1), jnp.stack(vss, axis=1))
```

```python
import functools
import math

import numpy as np
import jax
import jax.numpy as jnp
from jax import lax
from jax.experimental import pallas as pl
from jax.experimental.pallas import tpu as pltpu

F32 = jnp.float32
BF16 = jnp.bfloat16

HEAD_DIM = 64
WINDOW = 128
NUM_BUCKETS = 32
MAX_DISTANCE = 128
RMS_EPS = 1e-6
NEG_INF = -1e30
BLK = 128
TQ = 2 * BLK
LANES = 128
VMEM_LIMIT = 56 * 1024 * 1024

_NT = (((1,), (1,)), ((), ()))


def _rms(x, g):
    return x * lax.rsqrt(jnp.mean(x * x, axis=-1, keepdims=True) + RMS_EPS) * g


def _cparams(sem):
    return pltpu.CompilerParams(dimension_semantics=sem, vmem_limit_bytes=VMEM_LIMIT)


def _bucket_table():
    dist = np.arange(BLK)[:, None] + BLK - np.arange(2 * BLK)[None, :]
    n = np.maximum(dist, 0)
    max_exact = NUM_BUCKETS // 2
    nf = np.maximum(n, 1).astype(np.float32)
    large = max_exact + (np.log(nf / np.float32(max_exact)) / np.float32(math.log(MAX_DISTANCE / max_exact))
                         * np.float32(NUM_BUCKETS - max_exact)).astype(np.int32)
    large = np.minimum(large, NUM_BUCKETS - 1)
    return np.where(n < max_exact, n, large).astype(np.int32)


def _bias_kernel(table_ref, bucket_ref, causal_ref, band_ref):
    col = pl.program_id(0)
    bucket = bucket_ref[...]
    t = jnp.zeros(bucket.shape, F32)
    for b in range(NUM_BUCKETS):
        t = jnp.where(bucket == b, table_ref[b, col], t)
    far = table_ref[NUM_BUCKETS - 1, col]
    dist = (lax.broadcasted_iota(jnp.int32, bucket.shape, 0) + BLK
            - lax.broadcasted_iota(jnp.int32, bucket.shape, 1))
    causal_ref[...] = jnp.where(dist >= 0, t - far, NEG_INF)
    band_ref[...] = jnp.where((dist >= 0) & (dist < WINDOW), t, NEG_INF)


def _bias_tiles(rel_bias):
    n_cols = rel_bias.shape[1]
    bucket = jnp.asarray(_bucket_table())
    shape = jax.ShapeDtypeStruct((n_cols, BLK, 2 * BLK), F32)
    return pl.pallas_call(
        _bias_kernel,
        grid=(n_cols,),
        in_specs=[pl.BlockSpec(memory_space=pltpu.SMEM),
                  pl.BlockSpec((BLK, 2 * BLK), lambda c: (0, 0))],
        out_specs=[pl.BlockSpec((None, BLK, 2 * BLK), lambda c: (c, 0, 0)),
                   pl.BlockSpec((None, BLK, 2 * BLK), lambda c: (c, 0, 0))],
        out_shape=[shape, shape],
        compiler_params=_cparams(("arbitrary",)),
        name="bias_tiles",
    )(rel_bias, bucket)


def _qkv_kernel(x_ref, g_ref, w_ref, b_ref, q_ref, kvf_ref, kvb_ref, h_scr, *, n_q, chunk):
    h_scr[...] = _rms(x_ref[...], g_ref[...]).astype(BF16)
    n_total = w_ref.shape[1]
    for c0 in range(0, n_total, chunk):
        c1 = min(c0 + chunk, n_total)
        acc = jnp.dot(h_scr[...], w_ref[:, c0:c1], preferred_element_type=F32) + b_ref[:, c0:c1]
        if c1 <= n_q:
            q_ref[:, c0:c1] = acc.astype(BF16)
        else:
            kvf_ref[:, c0 - n_q:c1 - n_q] = acc
            kvb_ref[:, c0 - n_q:c1 - n_q] = acc.astype(BF16)


def _qkv_proj(x, g, w, layer, b, n_q, tm):
    m, d = x.shape
    n = w.shape[2]
    n_kv = n - n_q
    chunk = 256
    assert m % tm == 0 and n_q % chunk == 0 and n_kv % chunk == 0
    kern = functools.partial(_qkv_kernel, n_q=n_q, chunk=chunk)
    return pl.pallas_call(
        kern,
        grid=(m // tm,),
        in_specs=[pl.BlockSpec((tm, d), lambda i: (i, 0)),
                  pl.BlockSpec((1, d), lambda i: (0, 0)),
                  pl.BlockSpec((None, d, n), lambda i: (layer, 0, 0)),
                  pl.BlockSpec((1, n), lambda i: (0, 0))],
        out_specs=[pl.BlockSpec((tm, n_q), lambda i: (i, 0)),
                   pl.BlockSpec((tm, n_kv), lambda i: (i, 0)),
                   pl.BlockSpec((tm, n_kv), lambda i: (i, 0))],
        out_shape=[jax.ShapeDtypeStruct((m, n_q), BF16),
                   jax.ShapeDtypeStruct((m, n_kv), F32),
                   jax.ShapeDtypeStruct((m, n_kv), BF16)],
        scratch_shapes=[pltpu.VMEM((tm, d), BF16)],
        compiler_params=_cparams(("arbitrary",)),
        name="qkv_proj",
    )(x, g, w, b)


def _qkv_paged_kernel(x_ref, g_ref, w_ref, *rest, n_q, n_kv, chunk, first):
    q_ref, kvb_ref, ko_ref, vo_ref, h_scr = rest if first else rest[2:]
    tm = x_ref.shape[0]
    dv = ko_ref.shape[-1]
    h_scr[...] = _rms(x_ref[...], g_ref[...]).astype(BF16)
    n_total = w_ref.shape[1]
    for c0 in range(0, n_total, chunk):
        acc = jnp.dot(h_scr[...], w_ref[:, c0:c0 + chunk], preferred_element_type=F32)
        if c0 < n_q:
            q_ref[:, c0:c0 + chunk] = acc.astype(BF16)
            continue
        kvb_ref[:, c0 - n_q:c0 - n_q + chunk] = acc.astype(BF16)
        for j in range(chunk // dv):
            head = (c0 - n_q) // dv + j
            dst = ko_ref if head < n_kv else vo_ref
            rows = pl.ds(head % n_kv, tm, stride=n_kv)
            val = acc[:, j * dv:(j + 1) * dv]
            if first:
                for layer in range(dst.shape[0]):
                    dst[layer, rows, :] = val
            else:
                dst[rows, :] = val


def _qkv_proj_paged(x, g, w, n_q, n_kv, tm, batch, layer, prev):
    m, d = x.shape
    n_layers, _, n = w.shape
    seq = m // batch
    dv = (n - n_q) // (2 * n_kv)
    chunk = 256
    first = prev is None
    assert seq % tm == 0 and n_q % chunk == 0 and (n - n_q) % chunk == 0 and chunk % dv == 0
    tiles = seq // tm
    kern = functools.partial(_qkv_paged_kernel, n_q=n_q, n_kv=n_kv, chunk=chunk, first=first)
    cache_shape = jax.ShapeDtypeStruct((batch, n_layers, seq * n_kv, dv), F32)
    if first:
        cache_spec = pl.BlockSpec((None, n_layers, tm * n_kv, dv), lambda i: (i // tiles, 0, i % tiles, 0))
        extra_in, extra_args, aliases = [], [], {}
    else:
        cache_spec = pl.BlockSpec((None, None, tm * n_kv, dv), lambda i: (i // tiles, layer, i % tiles, 0))
        extra_in = [pl.BlockSpec(memory_space=pl.ANY)] * 2
        extra_args, aliases = list(prev), {3: 2, 4: 3}
    return pl.pallas_call(
        kern,
        grid=(m // tm,),
        in_specs=[pl.BlockSpec((tm, d), lambda i: (i, 0)),
                  pl.BlockSpec((1, d), lambda i: (0, 0)),
                  pl.BlockSpec((None, d, n), lambda i: (layer, 0, 0))] + extra_in,
        out_specs=[pl.BlockSpec((tm, n_q), lambda i: (i, 0)),
                   pl.BlockSpec((tm, n - n_q), lambda i: (i, 0)),
                   cache_spec, cache_spec],
        out_shape=[jax.ShapeDtypeStruct((m, n_q), BF16),
                   jax.ShapeDtypeStruct((m, n - n_q), BF16),
                   cache_shape, cache_shape],
        scratch_shapes=[pltpu.VMEM((tm, d), BF16)],
        input_output_aliases=aliases,
        compiler_params=_cparams(("arbitrary",)),
        name="qkv_proj_paged",
    )(x, g, w, *extra_args)


def _post_kernel(x_ref, o_ref, wo_ref, g_ref, wg_ref, wu_ref, wd_ref, gf_ref, out_ref,
                 h_scr, acc_scr, *, chunk, final):
    x1 = x_ref[...] + jnp.dot(o_ref[...], wo_ref[...], preferred_element_type=F32)
    acc_scr[...] = x1
    h_scr[...] = _rms(x1, g_ref[...]).astype(BF16)
    d_ff = wg_ref.shape[1]
    for c0 in range(0, d_ff, chunk):
        gate = jnp.dot(h_scr[...], wg_ref[:, c0:c0 + chunk], preferred_element_type=F32)
        up = jnp.dot(h_scr[...], wu_ref[:, c0:c0 + chunk], preferred_element_type=F32)
        act = (gate * (1.0 / (1.0 + jnp.exp(-gate))) * up).astype(BF16)
        acc_scr[...] += jnp.dot(act, wd_ref[c0:c0 + chunk, :], preferred_element_type=F32)
    if final:
        out_ref[...] = _rms(acc_scr[...], gf_ref[...])
    else:
        out_ref[...] = acc_scr[...]


def _post_attn(x, o, w_o, mix_layer, g, w_gate, w_up, w_down, layer, g_final, tm, final):
    m, d = x.shape
    d_o = o.shape[1]
    d_ff = w_gate.shape[2]
    chunk = 256
    assert m % tm == 0 and d_ff % chunk == 0
    kern = functools.partial(_post_kernel, chunk=chunk, final=final)

    def resident(shape, slot=None):
        if slot is None:
            return pl.BlockSpec(shape, lambda i: (0, 0), pipeline_mode=pl.Buffered(1))
        return pl.BlockSpec((None,) + shape, lambda i: (slot, 0, 0), pipeline_mode=pl.Buffered(1))

    return pl.pallas_call(
        kern,
        grid=(m // tm,),
        in_specs=[pl.BlockSpec((tm, d), lambda i: (i, 0)),
                  pl.BlockSpec((tm, d_o), lambda i: (i, 0)),
                  resident((d_o, d), mix_layer),
                  resident((1, d)),
                  resident((d, d_ff), layer),
                  resident((d, d_ff), layer),
                  resident((d_ff, d), layer),
                  resident((1, d))],
        out_specs=pl.BlockSpec((tm, d), lambda i: (i, 0)),
        out_shape=jax.ShapeDtypeStruct((m, d), F32),
        scratch_shapes=[pltpu.VMEM((tm, d), BF16), pltpu.VMEM((tm, d), F32)],
        compiler_params=_cparams(("arbitrary",)),
        name="post_attn",
    )(x, o, w_o, g, w_gate, w_up, w_down, g_final)


def _diff_lambda(lam_ref, lam_init):
    lp = lam_ref[...]
    s1 = jnp.sum(lp[0:1] * lp[1:2], axis=-1, keepdims=True)
    s2 = jnp.sum(lp[2:3] * lp[3:4], axis=-1, keepdims=True)
    return jnp.exp(s1) - jnp.exp(s2) + lam_init


def _near_bias(s, bias_ref, mp, n_g, diag):
    parts = []
    for g in range(n_g):
        tile = bias_ref[g * 2 + mp]
        t_prev, t_own = tile[:, :BLK], tile[:, BLK:]
        top = s[g * TQ:g * TQ + BLK]
        bot = s[g * TQ + BLK:(g + 1) * TQ]
        if diag:
            top = jnp.concatenate([top[:, :BLK] + t_own, jnp.full((BLK, BLK), NEG_INF, F32)], axis=1)
            bot = jnp.concatenate([bot[:, :BLK] + t_prev, bot[:, BLK:] + t_own], axis=1)
        else:
            top = jnp.concatenate([top[:, :BLK], top[:, BLK:] + t_prev], axis=1)
        parts += [top, bot]
    return jnp.concatenate(parts, axis=0)


def _diff_prompt_block(n_chunks, q_ref, k_ref, v_ref, bias_ref, lam_ref, subln_ref, o_ref, s_scr, m_scr, *,
                       lam_init, n_g):
    dv = v_ref.shape[1]
    lam = _diff_lambda(lam_ref, lam_init)
    o_maps = []
    for mp in range(2):
        q = jnp.concatenate([q_ref[:, (g * 2 + mp) * HEAD_DIM:(g * 2 + mp + 1) * HEAD_DIM]
                             for g in range(n_g)], axis=0) * 0.125
        ksl = slice(mp * HEAD_DIM, (mp + 1) * HEAD_DIM)

        mx = None
        for c in range(n_chunks):
            s = lax.dot_general(q, k_ref[c * TQ:(c + 1) * TQ, ksl], _NT, preferred_element_type=F32)
            if c >= n_chunks - 2:
                s = _near_bias(s, bias_ref, mp, n_g, diag=(c == n_chunks - 1))
            s_scr[:, c * TQ:(c + 1) * TQ] = s
            cm = jnp.maximum(s[:, :BLK], s[:, BLK:])
            mx = cm if mx is None else jnp.maximum(mx, cm)
        m_scr[...] = jnp.broadcast_to(jnp.max(mx, axis=1, keepdims=True), m_scr.shape)

        lsum = None
        acc = None
        for c in range(n_chunks):
            p0 = jnp.exp(s_scr[:, c * TQ:c * TQ + BLK] - m_scr[...])
            p1 = jnp.exp(s_scr[:, c * TQ + BLK:(c + 1) * TQ] - m_scr[...])
            pv = jnp.dot(jnp.concatenate([p0, p1], axis=1).astype(BF16), v_ref[c * TQ:(c + 1) * TQ, :],
                         preferred_element_type=F32)
            lsum = p0 + p1 if lsum is None else lsum + p0 + p1
            acc = pv if acc is None else acc + pv
        o_maps.append(acc / jnp.sum(lsum, axis=1, keepdims=True))

    d = o_maps[0] - lam * o_maps[1]
    d = d * lax.rsqrt(jnp.mean(d * d, axis=-1, keepdims=True) + RMS_EPS) * subln_ref[...]
    d = d * (1.0 - lam_init)
    for g in range(n_g):
        o_ref[:, g * dv:(g + 1) * dv] = d[g * TQ:(g + 1) * TQ].astype(o_ref.dtype)


def _diff_sample_step(grp, last, q_ref, knew_ref, vnew_ref, bias_ref, lam_ref, subln_ref, k_refs, v_refs, o_ref,
                      m_scr, l_scr, acc_scr, *, lam_init, n_kv):
    pages = len(k_refs)
    n_rows, dv = q_ref.shape
    n_heads = n_rows // 2
    prows = k_refs[0].shape[0]

    @pl.when(grp == 0)
    def _():
        m_scr[...] = jnp.full(m_scr.shape, NEG_INF, F32)
        l_scr[...] = jnp.zeros(l_scr.shape, F32)
        acc_scr[...] = jnp.zeros(acc_scr.shape, F32)

    qf = q_ref[...].astype(F32) * 0.125
    qb = qf.astype(BF16)

    def head_of_row(shape):
        return (lax.broadcasted_iota(jnp.int32, shape, 0) % n_heads) // (n_heads // n_kv)

    own = head_of_row((n_rows, prows)) == lax.broadcasted_iota(jnp.int32, (n_rows, prows), 1) % n_kv
    near_bias = jnp.where(grp == last, bias_ref[:, 0:prows], 0.0)
    s_list = []
    for p in range(pages):
        s = lax.dot_general(qb, k_refs[p][...].astype(BF16), _NT, preferred_element_type=F32)
        if p == pages - 1:
            s = s + near_bias
        s_list.append(jnp.where(own, s, NEG_INF))
    s_all = jnp.concatenate(s_list, axis=1)
    m_old = m_scr[...]
    m_new = jnp.maximum(m_old, jnp.max(s_all, axis=1, keepdims=True))
    alpha = jnp.exp(m_old - m_new)
    p_all = jnp.exp(s_all - m_new)
    l_new = alpha * l_scr[...] + jnp.sum(p_all, axis=1, keepdims=True)
    acc = alpha * acc_scr[...]
    for p in range(pages):
        acc = acc + jnp.dot(p_all[:, p * prows:(p + 1) * prows].astype(BF16), v_refs[p][...].astype(BF16),
                            preferred_element_type=F32)
    m_scr[...] = m_new
    l_scr[...] = l_new
    acc_scr[...] = acc

    @pl.when(grp == last)
    def _():
        row_kv = head_of_row((n_rows, dv))
        k_self = jnp.zeros((n_rows, dv), F32)
        v_self = jnp.zeros((n_rows, dv), F32)
        for kk in range(n_kv):
            k_self = k_self + jnp.where(row_kv == kk, knew_ref[:, kk * dv:(kk + 1) * dv], 0.0)
            v_self = v_self + jnp.where(row_kv == kk, vnew_ref[:, kk * dv:(kk + 1) * dv], 0.0)
        s_self = jnp.sum(qf * k_self, axis=1, keepdims=True) + bias_ref[:, prows:prows + 1]
        m_f = jnp.maximum(m_new, s_self)
        a = jnp.exp(m_new - m_f)
        p_self = jnp.exp(s_self - m_f)
        l_f = a * l_new + p_self
        o = (a * acc + p_self * v_self) / l_f
        lam = _diff_lambda(lam_ref, lam_init)
        d = o[:n_heads] - lam * o[n_heads:]
        d = d * lax.rsqrt(jnp.mean(d * d, axis=-1, keepdims=True) + RMS_EPS) * subln_ref[...]
        o_ref[...] = (d * (1.0 - lam_init)).astype(o_ref.dtype)


def _diff_attn_kernel(pt_ref, q_ref, k_ref, v_ref, bias_ref, lam_ref, subln_ref,
                      qs_ref, knew_ref, vnew_ref, bias_rows_ref, *rest,
                      pages, lam_init, n_g, n_kv, n_qblk, n_groups):
    k_refs = rest[:pages]
    v_refs = rest[pages:2 * pages]
    o_ref, os_ref, s_scr, m_scr, sm_scr, sl_scr, sacc_scr = rest[2 * pages:]
    step = pl.program_id(0)
    for blk in range(n_qblk):
        pl.when(step % n_qblk == blk)(functools.partial(
            _diff_prompt_block, blk + 1, q_ref, k_ref, v_ref, bias_ref, lam_ref, subln_ref, o_ref, s_scr, m_scr,
            lam_init=lam_init, n_g=n_g))
    _diff_sample_step(step % n_groups, n_groups - 1, qs_ref, knew_ref, vnew_ref, bias_rows_ref, lam_ref,
                      subln_ref, k_refs, v_refs, os_ref, sm_scr, sl_scr, sacc_scr, lam_init=lam_init, n_kv=n_kv)


def _diff_attn(q, kv, bias_causal, q_rows, k_new, v_new, bias_rows, cache_k, cache_v, li, page_table,
               lam_p, subln, lam_init, batch, seq, n_g):
    dec_b, n_rows, dv = q_rows.shape
    n_pool, n_layers, page, n_kv, _ = cache_k.shape
    n_pages = page_table.shape[1]
    n_heads = n_rows // 2
    width = n_kv * dv
    prows = page * n_kv
    qw = n_g * 2 * HEAD_DIM
    n_qblk = seq // TQ
    steps = batch * n_kv * n_qblk
    n_groups = steps // dec_b
    pages = n_pages // n_groups
    assert seq % TQ == 0 and steps % dec_b == 0 and n_pages % n_groups == 0
    q3 = q.reshape(batch, seq, q.shape[1])
    kv3 = kv.reshape(batch, seq, kv.shape[1])
    cache_k = cache_k.reshape(n_pool, n_layers, prows, dv)
    cache_v = cache_v.reshape(n_pool, n_layers, prows, dv)
    pt_flat = page_table.reshape(-1)

    def prompt_map(f):
        return lambda t, pt: f(t // (n_kv * n_qblk), (t // n_qblk) % n_kv, t % n_qblk)

    def seq_map(t, pt):
        return (t // n_groups, 0, 0)

    def page_map(p):
        return lambda t, pt: (pt[(t // n_groups) * n_pages + (t % n_groups) * pages + p], li, 0, 0)

    cache_specs = [pl.BlockSpec((None, None, prows, dv), page_map(p)) for p in range(pages)]
    kern = functools.partial(_diff_attn_kernel, pages=pages, lam_init=lam_init, n_g=n_g, n_kv=n_kv,
                             n_qblk=n_qblk, n_groups=n_groups)
    grid_spec = pltpu.PrefetchScalarGridSpec(
        num_scalar_prefetch=1,
        grid=(steps,),
        in_specs=[pl.BlockSpec((None, TQ, qw), prompt_map(lambda b, k, i: (b, i, k))),
                  pl.BlockSpec((None, seq, dv), prompt_map(lambda b, k, i: (b, 0, k))),
                  pl.BlockSpec((None, seq, dv), prompt_map(lambda b, k, i: (b, 0, n_kv + k))),
                  pl.BlockSpec((n_g * 2, BLK, 2 * BLK), prompt_map(lambda b, k, i: (k, 0, 0))),
                  pl.BlockSpec((4, HEAD_DIM), lambda t, pt: (0, 0)),
                  pl.BlockSpec((1, dv), lambda t, pt: (0, 0)),
                  pl.BlockSpec((None, n_rows, dv), seq_map),
                  pl.BlockSpec((None, 1, width), seq_map),
                  pl.BlockSpec((None, 1, width), seq_map),
                  pl.BlockSpec(bias_rows.shape, lambda t, pt: (0, 0))] + cache_specs + cache_specs,
        out_specs=[pl.BlockSpec((None, TQ, n_g * dv), prompt_map(lambda b, k, i: (b, i, k))),
                   pl.BlockSpec((None, n_heads, dv), seq_map)],
        scratch_shapes=[pltpu.VMEM((n_g * TQ, seq), F32), pltpu.VMEM((n_g * TQ, BLK), F32),
                        pltpu.VMEM((n_rows, 1), F32), pltpu.VMEM((n_rows, 1), F32),
                        pltpu.VMEM((n_rows, dv), F32)],
    )
    o, o_s = pl.pallas_call(
        kern,
        grid_spec=grid_spec,
        out_shape=[jax.ShapeDtypeStruct((batch, seq, n_kv * n_g * dv), BF16),
                   jax.ShapeDtypeStruct((dec_b, n_heads, dv), BF16)],
        compiler_params=_cparams(("arbitrary",)),
        name="diff_attn",
    )(pt_flat, q3, kv3, kv3, bias_causal, lam_p, subln,
      q_rows, k_new.reshape(dec_b, 1, width), v_new.reshape(dec_b, 1, width), bias_rows,
      *([cache_k] * pages), *([cache_v] * pages))
    return o.reshape(batch * seq, n_kv * n_g * dv), o_s.reshape(dec_b, n_heads * dv)


def _swa_prompt_kernel(sink_ref, q_ref, kvp_ref, kvc_ref, bias_ref, o_ref, *, n_kv, n_g):
    i = pl.program_id(1)
    col = lax.broadcasted_iota(jnp.int32, (BLK, 2 * BLK), 1)
    has_prev = (col >= BLK) | (i > 0)
    v_off = n_kv * HEAD_DIM
    outs = []
    for kv in range(n_kv):
        ksl = slice(kv * HEAD_DIM, (kv + 1) * HEAD_DIM)
        vsl = slice(v_off + kv * HEAD_DIM, v_off + (kv + 1) * HEAD_DIM)
        kk = jnp.concatenate([kvp_ref[:, ksl], kvc_ref[:, ksl]], axis=0)
        vv = jnp.concatenate([kvp_ref[:, vsl], kvc_ref[:, vsl]], axis=0)
        for g in range(n_g):
            h = kv * n_g + g
            q = q_ref[:, h * HEAD_DIM:(h + 1) * HEAD_DIM] * 0.125
            s = lax.dot_general(q, kk, _NT, preferred_element_type=F32) + bias_ref[h]
            s = jnp.where(has_prev, s, NEG_INF)
            sink = sink_ref[0, h]
            m = jnp.maximum(jnp.max(s, axis=1, keepdims=True), sink)
            p = jnp.exp(s - m)
            denom = jnp.sum(p, axis=1, keepdims=True) + jnp.exp(sink - m)
            o = jnp.dot(p.astype(BF16), vv, preferred_element_type=F32) / denom
            outs.append(o)
    o_ref[...] = jnp.concatenate(outs, axis=1).astype(o_ref.dtype)


def _swa_attn_prompt(q, kv, bias_band, sinks, batch, seq, n_kv, n_g):
    qw = q.shape[1]
    kvw = kv.shape[1]
    q3 = q.reshape(batch, seq, qw)
    kv3 = kv.reshape(batch, seq, kvw)
    kern = functools.partial(_swa_prompt_kernel, n_kv=n_kv, n_g=n_g)
    o = pl.pallas_call(
        kern,
        grid=(batch, seq // BLK),
        in_specs=[pl.BlockSpec(memory_space=pltpu.SMEM),
                  pl.BlockSpec((None, BLK, qw), lambda b, i: (b, i, 0)),
                  pl.BlockSpec((None, BLK, kvw), lambda b, i: (b, jnp.maximum(i - 1, 0), 0)),
                  pl.BlockSpec((None, BLK, kvw), lambda b, i: (b, i, 0)),
                  pl.BlockSpec((n_kv * n_g, BLK, 2 * BLK), lambda b, i: (0, 0, 0))],
        out_specs=pl.BlockSpec((None, BLK, qw), lambda b, i: (b, i, 0)),
        out_shape=jax.ShapeDtypeStruct((batch, seq, qw), BF16),
        compiler_params=_cparams(("arbitrary", "arbitrary")),
        name="swa_attn_prompt",
    )(sinks.reshape(1, -1), q3, kv3, kv3, bias_band)
    return o.reshape(batch * seq, qw)


def _swa_sample_kernel(q_ref, kvnew_ref, kcol_ref, sk_ref, sv_ref, bias_ref, sink_ref, o_ref, nk_ref, nv_ref,
                       *, n_kv, bb):
    n_heads = q_ref.shape[1]
    n_g = n_heads // n_kv
    buf = sk_ref.shape[-1]
    newest = lax.broadcasted_iota(jnp.int32, (HEAD_DIM, buf), 1) == buf - 1
    for s in range(bb):
        for kv in range(n_kv):
            hs = slice(kv * n_g, (kv + 1) * n_g)
            qf = q_ref[s, hs, :].astype(F32) * 0.125
            k_row = kvnew_ref[s, :, kv * HEAD_DIM:(kv + 1) * HEAD_DIM]
            v_row = kvnew_ref[s, :, (n_kv + kv) * HEAD_DIM:(n_kv + kv + 1) * HEAD_DIM]
            st_k = sk_ref[s, kv]
            st_v = sv_ref[s, kv]
            sink = sink_ref[hs, :]
            sc = jnp.dot(qf.astype(BF16), st_k.astype(BF16), preferred_element_type=F32) + bias_ref[hs, 0:BLK]
            s_self = jnp.sum(qf * k_row, axis=1, keepdims=True) + bias_ref[hs, BLK:BLK + 1]
            m = jnp.maximum(jnp.maximum(jnp.max(sc, axis=1, keepdims=True), s_self), sink)
            p = jnp.exp(sc - m)
            p_self = jnp.exp(s_self - m)
            denom = jnp.sum(p, axis=1, keepdims=True) + p_self + jnp.exp(sink - m)
            o = lax.dot_general(p.astype(BF16), st_v.astype(BF16), _NT, preferred_element_type=F32)
            o_ref[s, hs, :] = ((o + p_self * v_row) / denom).astype(o_ref.dtype)
            nk_ref[s, kv] = jnp.where(newest, kcol_ref[s, kv], pltpu.roll(st_k, buf - 1, axis=1))
            nv_ref[s, kv] = jnp.where(newest, kcol_ref[s, n_kv + kv], pltpu.roll(st_v, buf - 1, axis=1))


def _swa_attn_sample(q, kv_new, state_k_t, state_v_t, li, bias_rows, sinks, bb):
    b, n_heads, _ = q.shape
    n_kv, buf = state_k_t.shape[2], state_k_t.shape[4]
    kw = n_kv * HEAD_DIM
    assert buf == WINDOW and b % bb == 0
    kern = functools.partial(_swa_sample_kernel, n_kv=n_kv, bb=bb)
    st_spec = pl.BlockSpec((bb, None, n_kv, HEAD_DIM, buf), lambda i: (i, li, 0, 0, 0))
    new_spec = pl.BlockSpec((bb, n_kv, HEAD_DIM, buf), lambda i: (i, 0, 0, 0))
    new_shape = jax.ShapeDtypeStruct((b, n_kv, HEAD_DIM, buf), F32)
    o, nk, nv = pl.pallas_call(
        kern,
        grid=(b // bb,),
        in_specs=[pl.BlockSpec((bb, n_heads, HEAD_DIM), lambda i: (i, 0, 0)),
                  pl.BlockSpec((bb, 1, 2 * kw), lambda i: (i, 0, 0)),
                  pl.BlockSpec((bb, 2 * n_kv, HEAD_DIM, 1), lambda i: (i, 0, 0, 0)),
                  st_spec, st_spec,
                  pl.BlockSpec((n_heads, 2 * BLK), lambda i: (0, 0)),
                  pl.BlockSpec((n_heads, 1), lambda i: (0, 0))],
        out_specs=[pl.BlockSpec((bb, n_heads, HEAD_DIM), lambda i: (i, 0, 0)), new_spec, new_spec],
        out_shape=[jax.ShapeDtypeStruct((b, n_heads, HEAD_DIM), F32), new_shape, new_shape],
        compiler_params=_cparams(("arbitrary",)),
        name="swa_attn_sample",
    )(q, kv_new.reshape(b, 1, 2 * kw), kv_new.reshape(b, 2 * n_kv, HEAD_DIM, 1), state_k_t, state_v_t,
      bias_rows, sinks.reshape(-1, 1))
    return o.reshape(b, n_heads * HEAD_DIM).astype(BF16), nk, nv


def _block_rows(q_rows, n_kv_blocks, block_of_row):
    w = q_rows.shape[-1]
    tiled = jnp.tile(q_rows, (1, 1, n_kv_blocks))
    lane_block = np.arange(n_kv_blocks * w)[None, :] // w
    mask = jnp.asarray(lane_block == np.asarray(block_of_row)[:, None])
    return jnp.where(mask[None], tiled, jnp.zeros_like(tiled))


def kernel(x_prompt, x_sample, cache_k_diff, cache_v_diff, state_k_swa, state_v_swa, page_table,
           rel_bias, g_mix, g_ffn, g_final, w_qkv_diff, w_o_diff, lambda_diff, subln_diff,
           w_qkv_swa, b_qkv_swa, w_o_swa, sinks_swa, w_gate, w_up, w_down):
    batch, seq, d_model = x_prompt.shape
    dec_b = x_sample.shape[0]
    depth = g_mix.shape[0]
    n_pool, n_diff_layers, page, n_kv_diff, kd = cache_k_diff.shape
    n_heads_diff = w_o_diff.shape[1] // kd
    g_diff = n_heads_diff // n_kv_diff
    n_kv_swa = state_k_swa.shape[3]
    n_heads_swa = sinks_swa.shape[1]
    g_swa = n_heads_swa // n_kv_swa
    buf = state_k_swa.shape[2]
    q_diff = n_heads_diff * 2 * HEAD_DIM
    q_swa = n_heads_swa * HEAD_DIM
    kw_diff = n_kv_diff * kd
    kw_swa = n_kv_swa * HEAD_DIM

    bias_causal, bias_band = _bias_tiles(rel_bias)
    diff_cols = np.array([2 * (r % n_heads_diff) + r // n_heads_diff for r in range(2 * n_heads_diff)])
    diff_row_map = np.arange(2 * n_heads_diff) // n_heads_diff
    near = bias_causal[:, 0, :][diff_cols]
    bias_rows_diff = jnp.concatenate([jnp.repeat(near[:, :BLK], n_kv_diff, axis=1), near[:, BLK:]], axis=1)
    bias_rows_swa = bias_band[:, 0, :]

    st_k = state_k_swa.transpose(0, 1, 3, 4, 2)
    st_v = state_v_swa.transpose(0, 1, 3, 4, 2)

    xp = x_prompt.reshape(batch * seq, d_model)
    xs = x_sample.reshape(dec_b, d_model)
    tm_p, tm_s = 512, dec_b
    zero_bias = jnp.zeros((1, w_qkv_diff.shape[2]), F32)
    wq_diff, wo_diff = w_qkv_diff.astype(BF16), w_o_diff.astype(BF16)
    wq_swa, wo_swa = w_qkv_swa.astype(BF16), w_o_swa.astype(BF16)
    wg, wu, wd = w_gate.astype(BF16), w_up.astype(BF16), w_down.astype(BF16)
    kds, vds, ksp, vsp, kss, vss = [], [], [], [], [], []
    new_cache = None

    for i in range(depth):
        li = i // 2
        g_m = g_mix[i].reshape(1, d_model)
        if i % 2 == 0:
            lam_init = 0.8 - 0.6 * math.exp(-0.3 * i)
            w_o = wo_diff
            subln = subln_diff[li].reshape(1, kd)
            qp, kvp_b, k_cache, v_cache = _qkv_proj_paged(xp, g_m, wq_diff, q_diff, n_kv_diff, tm_p, batch, li,
                                                          new_cache)
            new_cache = (k_cache, v_cache)
            qs, kvs_f, _ = _qkv_proj(xs, g_m, wq_diff, li, zero_bias, q_diff, tm_s)
            q_rows = qs.reshape(dec_b, n_kv_diff, g_diff, 2, HEAD_DIM).transpose(0, 3, 1, 2, 4)
            q_rows = _block_rows(q_rows.reshape(dec_b, 2 * n_heads_diff, HEAD_DIM), 2, diff_row_map)
            op, os_ = _diff_attn(qp, kvp_b, bias_causal, q_rows, kvs_f[:, :kw_diff], kvs_f[:, kw_diff:],
                                 bias_rows_diff, cache_k_diff, cache_v_diff, li, page_table,
                                 lambda_diff[li], subln, lam_init, batch, seq, g_diff)
            kds.append(kvs_f[:, :kw_diff].reshape(dec_b, 1, n_kv_diff, kd))
            vds.append(kvs_f[:, kw_diff:].reshape(dec_b, 1, n_kv_diff, kd))
        else:
            w_o = wo_swa
            b_qkv = b_qkv_swa[li].reshape(1, -1)
            qp, kvp_f, kvp_b = _qkv_proj(xp, g_m, wq_swa, li, b_qkv, q_swa, tm_p)
            qs, kvs_f, _ = _qkv_proj(xs, g_m, wq_swa, li, b_qkv, q_swa, tm_s)
            op = _swa_attn_prompt(qp, kvp_b, bias_band, sinks_swa[li], batch, seq, n_kv_swa, g_swa)
            os_, nk, nv = _swa_attn_sample(qs.reshape(dec_b, n_heads_swa, HEAD_DIM).astype(F32), kvs_f,
                                           st_k, st_v, li, bias_rows_swa, sinks_swa[li], bb=8)
            tail = kvp_f.reshape(batch, seq, 2 * kw_swa)[:, seq - min(WINDOW, seq):]
            ksp.append(tail[..., :kw_swa].reshape(batch, -1, n_kv_swa, HEAD_DIM))
            vsp.append(tail[..., kw_swa:].reshape(batch, -1, n_kv_swa, HEAD_DIM))
            kss.append(nk.transpose(0, 3, 1, 2))
            vss.append(nv.transpose(0, 3, 1, 2))
        final = i == depth - 1
        g_f = g_ffn[i].reshape(1, d_model)
        gfin = g_final.reshape(1, d_model)
        xp = _post_attn(xp, op, w_o, li, g_f, wg, wu, wd, i, gfin, tm_p, final)
        xs = _post_attn(xs, os_, w_o, li, g_f, wg, wu, wd, i, gfin, tm_s, final)

    k_cache, v_cache = new_cache
    cache_shape = (batch, n_diff_layers, seq, n_kv_diff, kd)
    return (xp.reshape(batch, seq, d_model), xs.reshape(dec_b, 1, d_model),
            k_cache.reshape(cache_shape), v_cache.reshape(cache_shape),
            jnp.stack(kds, axis=1), jnp.stack(vds, axis=1),
            jnp.stack(ksp, axis=1), jnp.stack(vsp, axis=1), jnp.stack(kss, axis=1), jnp.stack(vss, axis=1))
```

```python
import functools
import math

import numpy as np
import jax
import jax.numpy as jnp
from jax import lax
from jax.experimental import pallas as pl
from jax.experimental.pallas import tpu as pltpu

F32 = jnp.float32
BF16 = jnp.bfloat16

HEAD_DIM = 64
WINDOW = 128
NUM_BUCKETS = 32
MAX_DISTANCE = 128
RMS_EPS = 1e-6
NEG_INF = -1e30
LOG2E = math.log2(math.e)
BLK = 128
TQ = 2 * BLK
LANES = 128
VMEM_LIMIT = 56 * 1024 * 1024

_NT = (((1,), (1,)), ((), ()))


def _rms(x, g):
    return x * lax.rsqrt(jnp.mean(x * x, axis=-1, keepdims=True) + RMS_EPS) * g


def _cparams(sem):
    return pltpu.CompilerParams(dimension_semantics=sem, vmem_limit_bytes=VMEM_LIMIT)


def _bucket_table():
    dist = np.arange(BLK)[:, None] + BLK - np.arange(2 * BLK)[None, :]
    n = np.maximum(dist, 0)
    max_exact = NUM_BUCKETS // 2
    nf = np.maximum(n, 1).astype(np.float32)
    large = max_exact + (np.log(nf / np.float32(max_exact)) / np.float32(math.log(MAX_DISTANCE / max_exact))
                         * np.float32(NUM_BUCKETS - max_exact)).astype(np.int32)
    large = np.minimum(large, NUM_BUCKETS - 1)
    return np.where(n < max_exact, n, large).astype(np.int32)


def _bias_kernel(table_ref, bucket_ref, causal_ref, causal2_ref, band_ref):
    col = pl.program_id(0)
    bucket = bucket_ref[...]
    t = jnp.zeros(bucket.shape, F32)
    for b in range(NUM_BUCKETS):
        t = jnp.where(bucket == b, table_ref[b, col], t)
    far = table_ref[NUM_BUCKETS - 1, col]
    dist = (lax.broadcasted_iota(jnp.int32, bucket.shape, 0) + BLK
            - lax.broadcasted_iota(jnp.int32, bucket.shape, 1))
    causal_ref[...] = jnp.where(dist >= 0, t - far, NEG_INF)
    causal2_ref[...] = jnp.where(dist >= 0, (t - far) * LOG2E, NEG_INF)
    band_ref[...] = jnp.where((dist >= 0) & (dist < WINDOW), t, NEG_INF)


def _bias_tiles(rel_bias):
    n_cols = rel_bias.shape[1]
    bucket = jnp.asarray(_bucket_table())
    shape = jax.ShapeDtypeStruct((n_cols, BLK, 2 * BLK), F32)
    return pl.pallas_call(
        _bias_kernel,
        grid=(n_cols,),
        in_specs=[pl.BlockSpec(memory_space=pltpu.SMEM),
                  pl.BlockSpec((BLK, 2 * BLK), lambda c: (0, 0))],
        out_specs=[pl.BlockSpec((None, BLK, 2 * BLK), lambda c: (c, 0, 0))] * 3,
        out_shape=[shape] * 3,
        compiler_params=_cparams(("arbitrary",)),
        name="bias_tiles",
    )(rel_bias, bucket)


def _qkv_kernel(x_ref, g_ref, w_ref, b_ref, q_ref, kvf_ref, kvb_ref, h_scr, *, n_q, chunk):
    h_scr[...] = _rms(x_ref[...], g_ref[...]).astype(BF16)
    n_total = w_ref.shape[1]
    for c0 in range(0, n_total, chunk):
        c1 = min(c0 + chunk, n_total)
        acc = jnp.dot(h_scr[...], w_ref[:, c0:c1], preferred_element_type=F32) + b_ref[:, c0:c1]
        if c1 <= n_q:
            q_ref[:, c0:c1] = acc.astype(BF16)
        else:
            kvf_ref[:, c0 - n_q:c1 - n_q] = acc
            kvb_ref[:, c0 - n_q:c1 - n_q] = acc.astype(BF16)


def _qkv_proj(x, g, w, layer, b, n_q, tm):
    m, d = x.shape
    n = w.shape[2]
    n_kv = n - n_q
    chunk = 256
    assert m % tm == 0 and n_q % chunk == 0 and n_kv % chunk == 0
    kern = functools.partial(_qkv_kernel, n_q=n_q, chunk=chunk)
    return pl.pallas_call(
        kern,
        grid=(m // tm,),
        in_specs=[pl.BlockSpec((tm, d), lambda i: (i, 0)),
                  pl.BlockSpec((1, d), lambda i: (0, 0)),
                  pl.BlockSpec((None, d, n), lambda i: (layer, 0, 0)),
                  pl.BlockSpec((1, n), lambda i: (0, 0))],
        out_specs=[pl.BlockSpec((tm, n_q), lambda i: (i, 0)),
                   pl.BlockSpec((tm, n_kv), lambda i: (i, 0)),
                   pl.BlockSpec((tm, n_kv), lambda i: (i, 0))],
        out_shape=[jax.ShapeDtypeStruct((m, n_q), BF16),
                   jax.ShapeDtypeStruct((m, n_kv), F32),
                   jax.ShapeDtypeStruct((m, n_kv), BF16)],
        scratch_shapes=[pltpu.VMEM((tm, d), BF16)],
        compiler_params=_cparams(("arbitrary",)),
        name="qkv_proj",
    )(x, g, w, b)


def _qkv_paged_kernel(x_ref, g_ref, w_ref, *rest, n_q, n_kv, chunk, first):
    q_ref, kvb_ref, ko_ref, vo_ref, h_scr = rest if first else rest[2:]
    tm = x_ref.shape[0]
    dv = ko_ref.shape[-1]
    h_scr[...] = _rms(x_ref[...], g_ref[...]).astype(BF16)
    n_total = w_ref.shape[1]
    for c0 in range(0, n_total, chunk):
        acc = jnp.dot(h_scr[...], w_ref[:, c0:c0 + chunk], preferred_element_type=F32)
        if c0 < n_q:
            q_ref[:, c0:c0 + chunk] = (acc * (HEAD_DIM ** -0.5 * LOG2E)).astype(BF16)
            continue
        kvb_ref[:, c0 - n_q:c0 - n_q + chunk] = acc.astype(BF16)
        for j in range(chunk // dv):
            head = (c0 - n_q) // dv + j
            dst = ko_ref if head < n_kv else vo_ref
            rows = pl.ds(head % n_kv, tm, stride=n_kv)
            val = acc[:, j * dv:(j + 1) * dv]
            if first:
                for layer in range(dst.shape[0]):
                    dst[layer, rows, :] = val
            else:
                dst[rows, :] = val


def _qkv_proj_paged(x, g, w, n_q, n_kv, tm, batch, layer, prev):
    m, d = x.shape
    n_layers, _, n = w.shape
    seq = m // batch
    dv = (n - n_q) // (2 * n_kv)
    chunk = 256
    first = prev is None
    assert seq % tm == 0 and n_q % chunk == 0 and (n - n_q) % chunk == 0 and chunk % dv == 0
    tiles = seq // tm
    kern = functools.partial(_qkv_paged_kernel, n_q=n_q, n_kv=n_kv, chunk=chunk, first=first)
    cache_shape = jax.ShapeDtypeStruct((batch, n_layers, seq * n_kv, dv), F32)
    if first:
        cache_spec = pl.BlockSpec((None, n_layers, tm * n_kv, dv), lambda i: (i // tiles, 0, i % tiles, 0))
        extra_in, extra_args, aliases = [], [], {}
    else:
        cache_spec = pl.BlockSpec((None, None, tm * n_kv, dv), lambda i: (i // tiles, layer, i % tiles, 0))
        extra_in = [pl.BlockSpec(memory_space=pl.ANY)] * 2
        extra_args, aliases = list(prev), {3: 2, 4: 3}
    return pl.pallas_call(
        kern,
        grid=(m // tm,),
        in_specs=[pl.BlockSpec((tm, d), lambda i: (i, 0)),
                  pl.BlockSpec((1, d), lambda i: (0, 0)),
                  pl.BlockSpec((None, d, n), lambda i: (layer, 0, 0))] + extra_in,
        out_specs=[pl.BlockSpec((tm, n_q), lambda i: (i, 0)),
                   pl.BlockSpec((tm, n - n_q), lambda i: (i, 0)),
                   cache_spec, cache_spec],
        out_shape=[jax.ShapeDtypeStruct((m, n_q), BF16),
                   jax.ShapeDtypeStruct((m, n - n_q), BF16),
                   cache_shape, cache_shape],
        scratch_shapes=[pltpu.VMEM((tm, d), BF16)],
        input_output_aliases=aliases,
        compiler_params=_cparams(("arbitrary",)),
        name="qkv_proj_paged",
    )(x, g, w, *extra_args)


def _post_kernel(x_ref, o_ref, wo_ref, g_ref, wg_ref, wu_ref, wd_ref, gf_ref, out_ref,
                 h_scr, acc_scr, *, chunk, final):
    x1 = x_ref[...] + jnp.dot(o_ref[...], wo_ref[...], preferred_element_type=F32)
    acc_scr[...] = x1
    h_scr[...] = _rms(x1, g_ref[...]).astype(BF16)
    d_ff = wg_ref.shape[1]
    for c0 in range(0, d_ff, chunk):
        gate = jnp.dot(h_scr[...], wg_ref[:, c0:c0 + chunk], preferred_element_type=F32)
        up = jnp.dot(h_scr[...], wu_ref[:, c0:c0 + chunk], preferred_element_type=F32)
        act = (gate * (1.0 / (1.0 + jnp.exp(-gate))) * up).astype(BF16)
        acc_scr[...] += jnp.dot(act, wd_ref[c0:c0 + chunk, :], preferred_element_type=F32)
    if final:
        out_ref[...] = _rms(acc_scr[...], gf_ref[...])
    else:
        out_ref[...] = acc_scr[...]


def _post_attn(x, o, w_o, mix_layer, g, w_gate, w_up, w_down, layer, g_final, tm, final):
    m, d = x.shape
    d_o = o.shape[1]
    d_ff = w_gate.shape[2]
    chunk = 256
    assert m % tm == 0 and d_ff % chunk == 0
    kern = functools.partial(_post_kernel, chunk=chunk, final=final)

    def resident(shape, slot=None):
        if slot is None:
            return pl.BlockSpec(shape, lambda i: (0, 0), pipeline_mode=pl.Buffered(1))
        return pl.BlockSpec((None,) + shape, lambda i: (slot, 0, 0), pipeline_mode=pl.Buffered(1))

    return pl.pallas_call(
        kern,
        grid=(m // tm,),
        in_specs=[pl.BlockSpec((tm, d), lambda i: (i, 0)),
                  pl.BlockSpec((tm, d_o), lambda i: (i, 0)),
                  resident((d_o, d), mix_layer),
                  resident((1, d)),
                  resident((d, d_ff), layer),
                  resident((d, d_ff), layer),
                  resident((d_ff, d), layer),
                  resident((1, d))],
        out_specs=pl.BlockSpec((tm, d), lambda i: (i, 0)),
        out_shape=jax.ShapeDtypeStruct((m, d), F32),
        scratch_shapes=[pltpu.VMEM((tm, d), BF16), pltpu.VMEM((tm, d), F32)],
        compiler_params=_cparams(("arbitrary",)),
        name="post_attn",
    )(x, o, w_o, g, w_gate, w_up, w_down, g_final)


def _diff_lambda(lam_ref, lam_init):
    lp = lam_ref[...]
    s1 = jnp.sum(lp[0:1] * lp[1:2], axis=-1, keepdims=True)
    s2 = jnp.sum(lp[2:3] * lp[3:4], axis=-1, keepdims=True)
    return jnp.exp(s1) - jnp.exp(s2) + lam_init


def _near_bias(s, bias_ref, mp, n_g, diag):
    parts = []
    for g in range(n_g):
        tile = bias_ref[g * 2 + mp]
        t_prev, t_own = tile[:, :BLK], tile[:, BLK:]
        top = s[g * TQ:g * TQ + BLK]
        bot = s[g * TQ + BLK:(g + 1) * TQ]
        if diag:
            top = jnp.concatenate([top[:, :BLK] + t_own, jnp.full((BLK, BLK), NEG_INF, F32)], axis=1)
            bot = jnp.concatenate([bot[:, :BLK] + t_prev, bot[:, BLK:] + t_own], axis=1)
        else:
            top = jnp.concatenate([top[:, :BLK], top[:, BLK:] + t_prev], axis=1)
        parts += [top, bot]
    return jnp.concatenate(parts, axis=0)


def _diff_prompt_block(n_chunks, q_ref, k_ref, v_ref, bias_ref, lam_ref, subln_ref, o_ref, s_scr, m_scr, v1_scr,
                       *, lam_init, n_g):
    dv = v_ref.shape[1]
    if n_chunks == 1:
        v1_scr[:, :dv] = v_ref[...]
        v1_scr[:, dv:] = jnp.ones((v1_scr.shape[0], v1_scr.shape[1] - dv), v1_scr.dtype)
    lam = _diff_lambda(lam_ref, lam_init)
    o_maps = []
    for mp in range(2):
        q = jnp.concatenate([q_ref[:, (g * 2 + mp) * HEAD_DIM:(g * 2 + mp + 1) * HEAD_DIM]
                             for g in range(n_g)], axis=0)
        ksl = slice(mp * HEAD_DIM, (mp + 1) * HEAD_DIM)

        mx = None
        for c in range(n_chunks):
            s = lax.dot_general(q, k_ref[c * TQ:(c + 1) * TQ, ksl], _NT, preferred_element_type=F32)
            if c >= n_chunks - 2:
                s = _near_bias(s, bias_ref, mp, n_g, diag=(c == n_chunks - 1))
            s_scr[:, c * TQ:(c + 1) * TQ] = s
            cm = jnp.maximum(s[:, :BLK], s[:, BLK:])
            mx = cm if mx is None else jnp.maximum(mx, cm)
        m_scr[...] = jnp.broadcast_to(jnp.max(mx, axis=1, keepdims=True), m_scr.shape)

        p = jnp.concatenate([jnp.exp2(s_scr[:, j * BLK:(j + 1) * BLK] - m_scr[...]).astype(BF16)
                             for j in range(n_chunks * TQ // BLK)], axis=1)
        acc = jnp.dot(p, v1_scr[0:n_chunks * TQ, :], preferred_element_type=F32)
        o_maps.append(acc[:, :dv] / acc[:, dv:dv + 1])

    d = o_maps[0] - lam * o_maps[1]
    d = d * lax.rsqrt(jnp.mean(d * d, axis=-1, keepdims=True) + RMS_EPS) * subln_ref[...]
    d = d * (1.0 - lam_init)
    for g in range(n_g):
        o_ref[:, g * dv:(g + 1) * dv] = d[g * TQ:(g + 1) * TQ].astype(o_ref.dtype)


def _head_of_row(shape, n_heads, n_kv):
    return (lax.broadcasted_iota(jnp.int32, shape, 0) % n_heads) // (n_heads // n_kv)


def _diff_sample_init(m_scr, l_scr, acc_scr):
    m_scr[...] = jnp.full(m_scr.shape, NEG_INF, F32)
    l_scr[...] = jnp.zeros(l_scr.shape, F32)
    acc_scr[...] = jnp.zeros(acc_scr.shape, F32)


def _diff_sample_update(grp, last, q_ref, bias_ref, k_refs, v_refs, m_scr, l_scr, acc_scr, *, n_kv):
    pages = len(k_refs)
    n_rows, dv = q_ref.shape
    n_heads = n_rows // 2
    prows = k_refs[0].shape[0]
    qb = (q_ref[...].astype(F32) * 0.125).astype(BF16)
    own = (_head_of_row((n_rows, prows), n_heads, n_kv)
           == lax.broadcasted_iota(jnp.int32, (n_rows, prows), 1) % n_kv)
    near_bias = jnp.where(grp == last, bias_ref[:, 0:prows], 0.0)
    s_list = []
    for p in range(pages):
        s = lax.dot_general(qb, k_refs[p][...].astype(BF16), _NT, preferred_element_type=F32)
        if p == pages - 1:
            s = s + near_bias
        s_list.append(jnp.where(own, s, NEG_INF))
    s_all = jnp.concatenate(s_list, axis=1)
    m_old = m_scr[...]
    m_new = jnp.maximum(m_old, jnp.max(s_all, axis=1, keepdims=True))
    alpha = jnp.exp(m_old - m_new)
    p_all = jnp.exp(s_all - m_new)
    l_new = alpha * l_scr[...] + jnp.sum(p_all, axis=1, keepdims=True)
    acc = alpha * acc_scr[...]
    for p in range(pages):
        acc = acc + jnp.dot(p_all[:, p * prows:(p + 1) * prows].astype(BF16), v_refs[p][...].astype(BF16),
                            preferred_element_type=F32)
    m_scr[...] = m_new
    l_scr[...] = l_new
    acc_scr[...] = acc


def _diff_sample_finish(q_ref, knew_ref, vnew_ref, bias_ref, lam_ref, subln_ref, o_ref, m_scr, l_scr, acc_scr, *,
                        lam_init, n_kv):
    n_rows, dv = q_ref.shape
    n_heads = n_rows // 2
    self_col = bias_ref.shape[1] - BLK
    qf = q_ref[...].astype(F32) * 0.125
    row_kv = _head_of_row((n_rows, dv), n_heads, n_kv)
    k_self = jnp.zeros((n_rows, dv), F32)
    v_self = jnp.zeros((n_rows, dv), F32)
    for kk in range(n_kv):
        k_self = k_self + jnp.where(row_kv == kk, knew_ref[:, kk * dv:(kk + 1) * dv], 0.0)
        v_self = v_self + jnp.where(row_kv == kk, vnew_ref[:, kk * dv:(kk + 1) * dv], 0.0)
    s_self = jnp.sum(qf * k_self, axis=1, keepdims=True) + bias_ref[:, self_col:self_col + 1]
    m_old = m_scr[...]
    m_f = jnp.maximum(m_old, s_self)
    a = jnp.exp(m_old - m_f)
    p_self = jnp.exp(s_self - m_f)
    l_f = a * l_scr[...] + p_self
    o = (a * acc_scr[...] + p_self * v_self) / l_f
    lam = _diff_lambda(lam_ref, lam_init)
    d = o[:n_heads] - lam * o[n_heads:]
    d = d * lax.rsqrt(jnp.mean(d * d, axis=-1, keepdims=True) + RMS_EPS) * subln_ref[...]
    o_ref[...] = (d * (1.0 - lam_init)).astype(o_ref.dtype)


def _diff_attn_kernel(pt_ref, q_ref, k_ref, v_ref, bias_ref, lam_ref, subln_ref,
                      qs_ref, knew_ref, vnew_ref, bias_rows_ref, *rest,
                      pages, lam_init, n_g, n_kv, n_qblk, n_groups):
    k_refs = rest[:pages]
    v_refs = rest[pages:2 * pages]
    o_ref, os_ref, s_scr, m_scr, v1_scr, sm_scr, sl_scr, sacc_scr = rest[2 * pages:]
    step = pl.program_id(0)
    grp = step % n_groups
    last = n_groups - 1
    sample_state = (sm_scr, sl_scr, sacc_scr)
    pl.when(grp == 0)(functools.partial(_diff_sample_init, *sample_state))

    def block(n_chunks):
        _diff_sample_update(grp, last, qs_ref, bias_rows_ref, k_refs, v_refs, *sample_state, n_kv=n_kv)
        _diff_prompt_block(n_chunks, q_ref, k_ref, v_ref, bias_ref, lam_ref, subln_ref, o_ref, s_scr, m_scr,
                           v1_scr, lam_init=lam_init, n_g=n_g)

    for blk in range(n_qblk):
        pl.when(step % n_qblk == blk)(functools.partial(block, blk + 1))
    pl.when(grp == last)(functools.partial(
        _diff_sample_finish, qs_ref, knew_ref, vnew_ref, bias_rows_ref, lam_ref, subln_ref, os_ref, *sample_state,
        lam_init=lam_init, n_kv=n_kv))


def _diff_attn(q, kv, bias_causal, q_rows, k_new, v_new, bias_rows, cache_k, cache_v, li, page_table,
               lam_p, subln, lam_init, batch, seq, n_g):
    dec_b, n_rows, dv = q_rows.shape
    n_pool, n_layers, page, n_kv, _ = cache_k.shape
    n_pages = page_table.shape[1]
    n_heads = n_rows // 2
    width = n_kv * dv
    prows = page * n_kv
    qw = n_g * 2 * HEAD_DIM
    n_qblk = seq // TQ
    steps = batch * n_kv * n_qblk
    n_groups = steps // dec_b
    pages = n_pages // n_groups
    assert seq % TQ == 0 and steps % dec_b == 0 and n_pages % n_groups == 0
    q3 = q.reshape(batch, seq, q.shape[1])
    kv3 = kv.reshape(batch, seq, kv.shape[1])
    cache_k = cache_k.reshape(n_pool, n_layers, prows, dv)
    cache_v = cache_v.reshape(n_pool, n_layers, prows, dv)
    pt_flat = page_table.reshape(-1)

    def prompt_map(f):
        return lambda t, pt: f(t // (n_kv * n_qblk), (t // n_qblk) % n_kv, t % n_qblk)

    def seq_map(t, pt):
        return (t // n_groups, 0, 0)

    def page_map(p):
        return lambda t, pt: (pt[(t // n_groups) * n_pages + (t % n_groups) * pages + p], li, 0, 0)

    cache_specs = [pl.BlockSpec((None, None, prows, dv), page_map(p)) for p in range(pages)]
    kern = functools.partial(_diff_attn_kernel, pages=pages, lam_init=lam_init, n_g=n_g, n_kv=n_kv,
                             n_qblk=n_qblk, n_groups=n_groups)
    grid_spec = pltpu.PrefetchScalarGridSpec(
        num_scalar_prefetch=1,
        grid=(steps,),
        in_specs=[pl.BlockSpec((None, TQ, qw), prompt_map(lambda b, k, i: (b, i, k))),
                  pl.BlockSpec((None, seq, dv), prompt_map(lambda b, k, i: (b, 0, k))),
                  pl.BlockSpec((None, seq, dv), prompt_map(lambda b, k, i: (b, 0, n_kv + k))),
                  pl.BlockSpec((n_g * 2, BLK, 2 * BLK), prompt_map(lambda b, k, i: (k, 0, 0))),
                  pl.BlockSpec((4, HEAD_DIM), lambda t, pt: (0, 0)),
                  pl.BlockSpec((1, dv), lambda t, pt: (0, 0)),
                  pl.BlockSpec((None, n_rows, dv), seq_map),
                  pl.BlockSpec((None, 1, width), seq_map),
                  pl.BlockSpec((None, 1, width), seq_map),
                  pl.BlockSpec(bias_rows.shape, lambda t, pt: (0, 0))] + cache_specs + cache_specs,
        out_specs=[pl.BlockSpec((None, TQ, n_g * dv), prompt_map(lambda b, k, i: (b, i, k))),
                   pl.BlockSpec((None, n_heads, dv), seq_map)],
        scratch_shapes=[pltpu.VMEM((n_g * TQ, seq), F32), pltpu.VMEM((n_g * TQ, BLK), F32),
                        pltpu.VMEM((seq, 2 * dv), BF16),
                        pltpu.VMEM((n_rows, 1), F32), pltpu.VMEM((n_rows, 1), F32),
                        pltpu.VMEM((n_rows, dv), F32)],
    )
    o, o_s = pl.pallas_call(
        kern,
        grid_spec=grid_spec,
        out_shape=[jax.ShapeDtypeStruct((batch, seq, n_kv * n_g * dv), BF16),
                   jax.ShapeDtypeStruct((dec_b, n_heads, dv), BF16)],
        compiler_params=_cparams(("arbitrary",)),
        name="diff_attn",
    )(pt_flat, q3, kv3, kv3, bias_causal, lam_p, subln,
      q_rows, k_new.reshape(dec_b, 1, width), v_new.reshape(dec_b, 1, width), bias_rows,
      *([cache_k] * pages), *([cache_v] * pages))
    return o.reshape(batch * seq, n_kv * n_g * dv), o_s.reshape(dec_b, n_heads * dv)


def _swa_prompt_kernel(sink_ref, q_ref, kvp_ref, kvc_ref, bias_ref, o_ref, *, n_kv, n_g):
    i = pl.program_id(1)
    col = lax.broadcasted_iota(jnp.int32, (BLK, 2 * BLK), 1)
    has_prev = (col >= BLK) | (i > 0)
    v_off = n_kv * HEAD_DIM
    outs = []
    for kv in range(n_kv):
        ksl = slice(kv * HEAD_DIM, (kv + 1) * HEAD_DIM)
        vsl = slice(v_off + kv * HEAD_DIM, v_off + (kv + 1) * HEAD_DIM)
        kk = jnp.concatenate([kvp_ref[:, ksl], kvc_ref[:, ksl]], axis=0)
        vv = jnp.concatenate([kvp_ref[:, vsl], kvc_ref[:, vsl]], axis=0)
        for g in range(n_g):
            h = kv * n_g + g
            q = q_ref[:, h * HEAD_DIM:(h + 1) * HEAD_DIM] * 0.125
            s = lax.dot_general(q, kk, _NT, preferred_element_type=F32) + bias_ref[h]
            s = jnp.where(has_prev, s, NEG_INF)
            sink = sink_ref[0, h]
            m = jnp.maximum(jnp.max(s, axis=1, keepdims=True), sink)
            p = jnp.exp(s - m)
            denom = jnp.sum(p, axis=1, keepdims=True) + jnp.exp(sink - m)
            o = jnp.dot(p.astype(BF16), vv, preferred_element_type=F32) / denom
            outs.append(o)
    o_ref[...] = jnp.concatenate(outs, axis=1).astype(o_ref.dtype)


def _swa_attn_prompt(q, kv, bias_band, sinks, batch, seq, n_kv, n_g):
    qw = q.shape[1]
    kvw = kv.shape[1]
    q3 = q.reshape(batch, seq, qw)
    kv3 = kv.reshape(batch, seq, kvw)
    kern = functools.partial(_swa_prompt_kernel, n_kv=n_kv, n_g=n_g)
    o = pl.pallas_call(
        kern,
        grid=(batch, seq // BLK),
        in_specs=[pl.BlockSpec(memory_space=pltpu.SMEM),
                  pl.BlockSpec((None, BLK, qw), lambda b, i: (b, i, 0)),
                  pl.BlockSpec((None, BLK, kvw), lambda b, i: (b, jnp.maximum(i - 1, 0), 0)),
                  pl.BlockSpec((None, BLK, kvw), lambda b, i: (b, i, 0)),
                  pl.BlockSpec((n_kv * n_g, BLK, 2 * BLK), lambda b, i: (0, 0, 0))],
        out_specs=pl.BlockSpec((None, BLK, qw), lambda b, i: (b, i, 0)),
        out_shape=jax.ShapeDtypeStruct((batch, seq, qw), BF16),
        compiler_params=_cparams(("arbitrary", "arbitrary")),
        name="swa_attn_prompt",
    )(sinks.reshape(1, -1), q3, kv3, kv3, bias_band)
    return o.reshape(batch * seq, qw)


def _swa_sample_kernel(q_ref, kvnew_ref, kcol_ref, sk_ref, sv_ref, bias_ref, sink_ref, o_ref, nk_ref, nv_ref,
                       *, n_kv, bb):
    n_heads = q_ref.shape[1]
    n_g = n_heads // n_kv
    buf = sk_ref.shape[-1]
    newest = lax.broadcasted_iota(jnp.int32, (HEAD_DIM, buf), 1) == buf - 1
    for s in range(bb):
        for kv in range(n_kv):
            hs = slice(kv * n_g, (kv + 1) * n_g)
            qf = q_ref[s, hs, :].astype(F32) * 0.125
            k_row = kvnew_ref[s, :, kv * HEAD_DIM:(kv + 1) * HEAD_DIM]
            v_row = kvnew_ref[s, :, (n_kv + kv) * HEAD_DIM:(n_kv + kv + 1) * HEAD_DIM]
            st_k = sk_ref[s, kv]
            st_v = sv_ref[s, kv]
            sink = sink_ref[hs, :]
            sc = jnp.dot(qf.astype(BF16), st_k.astype(BF16), preferred_element_type=F32) + bias_ref[hs, 0:BLK]
            s_self = jnp.sum(qf * k_row, axis=1, keepdims=True) + bias_ref[hs, BLK:BLK + 1]
            m = jnp.maximum(jnp.maximum(jnp.max(sc, axis=1, keepdims=True), s_self), sink)
            p = jnp.exp(sc - m)
            p_self = jnp.exp(s_self - m)
            denom = jnp.sum(p, axis=1, keepdims=True) + p_self + jnp.exp(sink - m)
            o = lax.dot_general(p.astype(BF16), st_v.astype(BF16), _NT, preferred_element_type=F32)
            o_ref[s, hs, :] = ((o + p_self * v_row) / denom).astype(o_ref.dtype)
            nk_ref[s, kv] = jnp.where(newest, kcol_ref[s, kv], pltpu.roll(st_k, buf - 1, axis=1))
            nv_ref[s, kv] = jnp.where(newest, kcol_ref[s, n_kv + kv], pltpu.roll(st_v, buf - 1, axis=1))


def _swa_attn_sample(q, kv_new, state_k_t, state_v_t, li, bias_rows, sinks, bb):
    b, n_heads, _ = q.shape
    n_kv, buf = state_k_t.shape[2], state_k_t.shape[4]
    kw = n_kv * HEAD_DIM
    assert buf == WINDOW and b % bb == 0
    kern = functools.partial(_swa_sample_kernel, n_kv=n_kv, bb=bb)
    st_spec = pl.BlockSpec((bb, None, n_kv, HEAD_DIM, buf), lambda i: (i, li, 0, 0, 0))
    new_spec = pl.BlockSpec((bb, n_kv, HEAD_DIM, buf), lambda i: (i, 0, 0, 0))
    new_shape = jax.ShapeDtypeStruct((b, n_kv, HEAD_DIM, buf), F32)
    o, nk, nv = pl.pallas_call(
        kern,
        grid=(b // bb,),
        in_specs=[pl.BlockSpec((bb, n_heads, HEAD_DIM), lambda i: (i, 0, 0)),
                  pl.BlockSpec((bb, 1, 2 * kw), lambda i: (i, 0, 0)),
                  pl.BlockSpec((bb, 2 * n_kv, HEAD_DIM, 1), lambda i: (i, 0, 0, 0)),
                  st_spec, st_spec,
                  pl.BlockSpec((n_heads, 2 * BLK), lambda i: (0, 0)),
                  pl.BlockSpec((n_heads, 1), lambda i: (0, 0))],
        out_specs=[pl.BlockSpec((bb, n_heads, HEAD_DIM), lambda i: (i, 0, 0)), new_spec, new_spec],
        out_shape=[jax.ShapeDtypeStruct((b, n_heads, HEAD_DIM), F32), new_shape, new_shape],
        compiler_params=_cparams(("arbitrary",)),
        name="swa_attn_sample",
    )(q, kv_new.reshape(b, 1, 2 * kw), kv_new.reshape(b, 2 * n_kv, HEAD_DIM, 1), state_k_t, state_v_t,
      bias_rows, sinks.reshape(-1, 1))
    return o.reshape(b, n_heads * HEAD_DIM).astype(BF16), nk, nv


def _block_rows(q_rows, n_kv_blocks, block_of_row):
    w = q_rows.shape[-1]
    tiled = jnp.tile(q_rows, (1, 1, n_kv_blocks))
    lane_block = np.arange(n_kv_blocks * w)[None, :] // w
    mask = jnp.asarray(lane_block == np.asarray(block_of_row)[:, None])
    return jnp.where(mask[None], tiled, jnp.zeros_like(tiled))


def kernel(x_prompt, x_sample, cache_k_diff, cache_v_diff, state_k_swa, state_v_swa, page_table,
           rel_bias, g_mix, g_ffn, g_final, w_qkv_diff, w_o_diff, lambda_diff, subln_diff,
           w_qkv_swa, b_qkv_swa, w_o_swa, sinks_swa, w_gate, w_up, w_down):
    batch, seq, d_model = x_prompt.shape
    dec_b = x_sample.shape[0]
    depth = g_mix.shape[0]
    n_pool, n_diff_layers, page, n_kv_diff, kd = cache_k_diff.shape
    n_heads_diff = w_o_diff.shape[1] // kd
    g_diff = n_heads_diff // n_kv_diff
    n_kv_swa = state_k_swa.shape[3]
    n_heads_swa = sinks_swa.shape[1]
    g_swa = n_heads_swa // n_kv_swa
    buf = state_k_swa.shape[2]
    q_diff = n_heads_diff * 2 * HEAD_DIM
    q_swa = n_heads_swa * HEAD_DIM
    kw_diff = n_kv_diff * kd
    kw_swa = n_kv_swa * HEAD_DIM

    bias_causal, bias_causal2, bias_band = _bias_tiles(rel_bias)
    diff_cols = np.array([2 * (r % n_heads_diff) + r // n_heads_diff for r in range(2 * n_heads_diff)])
    diff_row_map = np.arange(2 * n_heads_diff) // n_heads_diff
    near = bias_causal[:, 0, :][diff_cols]
    bias_rows_diff = jnp.concatenate([jnp.repeat(near[:, :BLK], n_kv_diff, axis=1), near[:, BLK:]], axis=1)
    bias_rows_swa = bias_band[:, 0, :]

    st_k = state_k_swa.transpose(0, 1, 3, 4, 2)
    st_v = state_v_swa.transpose(0, 1, 3, 4, 2)

    xp = x_prompt.reshape(batch * seq, d_model)
    xs = x_sample.reshape(dec_b, d_model)
    tm_p, tm_s = 512, dec_b
    zero_bias = jnp.zeros((1, w_qkv_diff.shape[2]), F32)
    wq_diff, wo_diff = w_qkv_diff.astype(BF16), w_o_diff.astype(BF16)
    wq_swa, wo_swa = w_qkv_swa.astype(BF16), w_o_swa.astype(BF16)
    wg, wu, wd = w_gate.astype(BF16), w_up.astype(BF16), w_down.astype(BF16)
    kds, vds, ksp, vsp, kss, vss = [], [], [], [], [], []
    new_cache = None

    for i in range(depth):
        li = i // 2
        g_m = g_mix[i].reshape(1, d_model)
        if i % 2 == 0:
            lam_init = 0.8 - 0.6 * math.exp(-0.3 * i)
            w_o = wo_diff
            subln = subln_diff[li].reshape(1, kd)
            qp, kvp_b, k_cache, v_cache = _qkv_proj_paged(xp, g_m, wq_diff, q_diff, n_kv_diff, tm_p, batch, li,
                                                          new_cache)
            new_cache = (k_cache, v_cache)
            qs, kvs_f, _ = _qkv_proj(xs, g_m, wq_diff, li, zero_bias, q_diff, tm_s)
            q_rows = qs.reshape(dec_b, n_kv_diff, g_diff, 2, HEAD_DIM).transpose(0, 3, 1, 2, 4)
            q_rows = _block_rows(q_rows.reshape(dec_b, 2 * n_heads_diff, HEAD_DIM), 2, diff_row_map)
            op, os_ = _diff_attn(qp, kvp_b, bias_causal2, q_rows, kvs_f[:, :kw_diff], kvs_f[:, kw_diff:],
                                 bias_rows_diff, cache_k_diff, cache_v_diff, li, page_table,
                                 lambda_diff[li], subln, lam_init, batch, seq, g_diff)
            kds.append(kvs_f[:, :kw_diff].reshape(dec_b, 1, n_kv_diff, kd))
            vds.append(kvs_f[:, kw_diff:].reshape(dec_b, 1, n_kv_diff, kd))
        else:
            w_o = wo_swa
            b_qkv = b_qkv_swa[li].reshape(1, -1)
            qp, kvp_f, kvp_b = _qkv_proj(xp, g_m, wq_swa, li, b_qkv, q_swa, tm_p)
            qs, kvs_f, _ = _qkv_proj(xs, g_m, wq_swa, li, b_qkv, q_swa, tm_s)
            op = _swa_attn_prompt(qp, kvp_b, bias_band, sinks_swa[li], batch, seq, n_kv_swa, g_swa)
            os_, nk, nv = _swa_attn_sample(qs.reshape(dec_b, n_heads_swa, HEAD_DIM).astype(F32), kvs_f,
                                           st_k, st_v, li, bias_rows_swa, sinks_swa[li], bb=8)
            tail = kvp_f.reshape(batch, seq, 2 * kw_swa)[:, seq - min(WINDOW, seq):]
            ksp.append(tail[..., :kw_swa].reshape(batch, -1, n_kv_swa, HEAD_DIM))
            vsp.append(tail[..., kw_swa:].reshape(batch, -1, n_kv_swa, HEAD_DIM))
            kss.append(nk.transpose(0, 3, 1, 2))
            vss.append(nv.transpose(0, 3, 1, 2))
        final = i == depth - 1
        g_f = g_ffn[i].reshape(1, d_model)
        gfin = g_final.reshape(1, d_model)
        xp = _post_attn(xp, op, w_o, li, g_f, wg, wu, wd, i, gfin, tm_p, final)
        xs = _post_attn(xs, os_, w_o, li, g_f, wg, wu, wd, i, gfin, tm_s, final)

    k_cache, v_cache = new_cache
    cache_shape = (batch, n_diff_layers, seq, n_kv_diff, kd)
    return (xp.reshape(batch, seq, d_model), xs.reshape(dec_b, 1, d_model),
            k_cache.reshape(cache_shape), v_cache.reshape(cache_shape),
            jnp.stack(kds, axis=1), jnp.stack(vds, axis=1),
            jnp.stack(ksp, axis=1), jnp.stack(vsp, axis=1), jnp.stack(kss, axis=1), jnp.stack(vss, axis=1))
```

```python
import functools
import math

import numpy as np
import jax
import jax.numpy as jnp
from jax import lax
from jax.experimental import pallas as pl
from jax.experimental.pallas import tpu as pltpu

F32 = jnp.float32
BF16 = jnp.bfloat16

HEAD_DIM = 64
WINDOW = 128
NUM_BUCKETS = 32
MAX_DISTANCE = 128
RMS_EPS = 1e-6
NEG_INF = -1e30
LOG2E = math.log2(math.e)
BLK = 128
TQ = 2 * BLK
LANES = 128
VMEM_LIMIT = 56 * 1024 * 1024

_NT = (((1,), (1,)), ((), ()))


def _rms(x, g):
    return x * lax.rsqrt(jnp.mean(x * x, axis=-1, keepdims=True) + RMS_EPS) * g


def _cparams(sem):
    return pltpu.CompilerParams(dimension_semantics=sem, vmem_limit_bytes=VMEM_LIMIT)


def _bucket_table():
    dist = np.arange(BLK)[:, None] + BLK - np.arange(2 * BLK)[None, :]
    n = np.maximum(dist, 0)
    max_exact = NUM_BUCKETS // 2
    nf = np.maximum(n, 1).astype(np.float32)
    large = max_exact + (np.log(nf / np.float32(max_exact)) / np.float32(math.log(MAX_DISTANCE / max_exact))
                         * np.float32(NUM_BUCKETS - max_exact)).astype(np.int32)
    large = np.minimum(large, NUM_BUCKETS - 1)
    return np.where(n < max_exact, n, large).astype(np.int32)


def _bias_kernel(table_ref, bucket_ref, causal_ref, causal2_ref, band_ref):
    col = pl.program_id(0)
    bucket = bucket_ref[...]
    t = jnp.zeros(bucket.shape, F32)
    for b in range(NUM_BUCKETS):
        t = jnp.where(bucket == b, table_ref[b, col], t)
    far = table_ref[NUM_BUCKETS - 1, col]
    dist = (lax.broadcasted_iota(jnp.int32, bucket.shape, 0) + BLK
            - lax.broadcasted_iota(jnp.int32, bucket.shape, 1))
    causal_ref[...] = jnp.where(dist >= 0, t - far, NEG_INF)
    causal2_ref[...] = jnp.where(dist >= 0, (t - far) * LOG2E, NEG_INF)
    band_ref[...] = jnp.where((dist >= 0) & (dist < WINDOW), t, NEG_INF)


def _bias_tiles(rel_bias):
    n_cols = rel_bias.shape[1]
    bucket = jnp.asarray(_bucket_table())
    shape = jax.ShapeDtypeStruct((n_cols, BLK, 2 * BLK), F32)
    return pl.pallas_call(
        _bias_kernel,
        grid=(n_cols,),
        in_specs=[pl.BlockSpec(memory_space=pltpu.SMEM),
                  pl.BlockSpec((BLK, 2 * BLK), lambda c: (0, 0))],
        out_specs=[pl.BlockSpec((None, BLK, 2 * BLK), lambda c: (c, 0, 0))] * 3,
        out_shape=[shape] * 3,
        compiler_params=_cparams(("arbitrary",)),
        name="bias_tiles",
    )(rel_bias, bucket)


def _qkv_kernel(x_ref, g_ref, w_ref, b_ref, q_ref, kvf_ref, kvb_ref, h_scr, *, n_q, chunk):
    h_scr[...] = _rms(x_ref[...], g_ref[...]).astype(BF16)
    n_total = w_ref.shape[1]
    for c0 in range(0, n_total, chunk):
        c1 = min(c0 + chunk, n_total)
        acc = jnp.dot(h_scr[...], w_ref[:, c0:c1], preferred_element_type=F32) + b_ref[:, c0:c1]
        if c1 <= n_q:
            q_ref[:, c0:c1] = acc.astype(BF16)
        else:
            kvf_ref[:, c0 - n_q:c1 - n_q] = acc
            kvb_ref[:, c0 - n_q:c1 - n_q] = acc.astype(BF16)


def _qkv_proj(x, g, w, layer, b, n_q, tm):
    m, d = x.shape
    n = w.shape[2]
    n_kv = n - n_q
    chunk = 256
    assert m % tm == 0 and n_q % chunk == 0 and n_kv % chunk == 0
    kern = functools.partial(_qkv_kernel, n_q=n_q, chunk=chunk)
    return pl.pallas_call(
        kern,
        grid=(m // tm,),
        in_specs=[pl.BlockSpec((tm, d), lambda i: (i, 0)),
                  pl.BlockSpec((1, d), lambda i: (0, 0)),
                  pl.BlockSpec((None, d, n), lambda i: (layer, 0, 0)),
                  pl.BlockSpec((1, n), lambda i: (0, 0))],
        out_specs=[pl.BlockSpec((tm, n_q), lambda i: (i, 0)),
                   pl.BlockSpec((tm, n_kv), lambda i: (i, 0)),
                   pl.BlockSpec((tm, n_kv), lambda i: (i, 0))],
        out_shape=[jax.ShapeDtypeStruct((m, n_q), BF16),
                   jax.ShapeDtypeStruct((m, n_kv), F32),
                   jax.ShapeDtypeStruct((m, n_kv), BF16)],
        scratch_shapes=[pltpu.VMEM((tm, d), BF16)],
        compiler_params=_cparams(("arbitrary",)),
        name="qkv_proj",
    )(x, g, w, b)


def _qkv_paged_kernel(x_ref, g_ref, w_ref, *rest, n_q, n_kv, chunk, first):
    q_ref, kvb_ref, ko_ref, vo_ref, h_scr = rest if first else rest[2:]
    tm = x_ref.shape[0]
    dv = ko_ref.shape[-1]
    h_scr[...] = _rms(x_ref[...], g_ref[...]).astype(BF16)
    n_total = w_ref.shape[1]
    for c0 in range(0, n_total, chunk):
        acc = jnp.dot(h_scr[...], w_ref[:, c0:c0 + chunk], preferred_element_type=F32)
        if c0 < n_q:
            q_ref[:, c0:c0 + chunk] = (acc * (HEAD_DIM ** -0.5 * LOG2E)).astype(BF16)
            continue
        kvb_ref[:, c0 - n_q:c0 - n_q + chunk] = acc.astype(BF16)
        for j in range(chunk // dv):
            head = (c0 - n_q) // dv + j
            dst = ko_ref if head < n_kv else vo_ref
            rows = pl.ds(head % n_kv, tm, stride=n_kv)
            val = acc[:, j * dv:(j + 1) * dv]
            if first:
                for layer in range(dst.shape[0]):
                    dst[layer, rows, :] = val
            else:
                dst[rows, :] = val


def _qkv_proj_paged(x, g, w, n_q, n_kv, tm, batch, layer, prev):
    m, d = x.shape
    n_layers, _, n = w.shape
    seq = m // batch
    dv = (n - n_q) // (2 * n_kv)
    chunk = 256
    first = prev is None
    assert seq % tm == 0 and n_q % chunk == 0 and (n - n_q) % chunk == 0 and chunk % dv == 0
    tiles = seq // tm
    kern = functools.partial(_qkv_paged_kernel, n_q=n_q, n_kv=n_kv, chunk=chunk, first=first)
    cache_shape = jax.ShapeDtypeStruct((batch, n_layers, seq * n_kv, dv), F32)
    if first:
        cache_spec = pl.BlockSpec((None, n_layers, tm * n_kv, dv), lambda i: (i // tiles, 0, i % tiles, 0))
        extra_in, extra_args, aliases = [], [], {}
    else:
        cache_spec = pl.BlockSpec((None, None, tm * n_kv, dv), lambda i: (i // tiles, layer, i % tiles, 0))
        extra_in = [pl.BlockSpec(memory_space=pl.ANY)] * 2
        extra_args, aliases = list(prev), {3: 2, 4: 3}
    return pl.pallas_call(
        kern,
        grid=(m // tm,),
        in_specs=[pl.BlockSpec((tm, d), lambda i: (i, 0)),
                  pl.BlockSpec((1, d), lambda i: (0, 0)),
                  pl.BlockSpec((None, d, n), lambda i: (layer, 0, 0))] + extra_in,
        out_specs=[pl.BlockSpec((tm, n_q), lambda i: (i, 0)),
                   pl.BlockSpec((tm, n - n_q), lambda i: (i, 0)),
                   cache_spec, cache_spec],
        out_shape=[jax.ShapeDtypeStruct((m, n_q), BF16),
                   jax.ShapeDtypeStruct((m, n - n_q), BF16),
                   cache_shape, cache_shape],
        scratch_shapes=[pltpu.VMEM((tm, d), BF16)],
        input_output_aliases=aliases,
        compiler_params=_cparams(("arbitrary",)),
        name="qkv_proj_paged",
    )(x, g, w, *extra_args)


def _post_kernel(x_ref, o_ref, wo_ref, g_ref, wg_ref, wu_ref, wd_ref, gf_ref, out_ref,
                 h_scr, acc_scr, *, chunk, final):
    x1 = x_ref[...] + jnp.dot(o_ref[...], wo_ref[...], preferred_element_type=F32)
    acc_scr[...] = x1
    h_scr[...] = _rms(x1, g_ref[...]).astype(BF16)
    d_ff = wg_ref.shape[1]
    for c0 in range(0, d_ff, chunk):
        gate = jnp.dot(h_scr[...], wg_ref[:, c0:c0 + chunk], preferred_element_type=F32)
        up = jnp.dot(h_scr[...], wu_ref[:, c0:c0 + chunk], preferred_element_type=F32)
        act = (gate * (1.0 / (1.0 + jnp.exp(-gate))) * up).astype(BF16)
        acc_scr[...] += jnp.dot(act, wd_ref[c0:c0 + chunk, :], preferred_element_type=F32)
    if final:
        out_ref[...] = _rms(acc_scr[...], gf_ref[...])
    else:
        out_ref[...] = acc_scr[...]


def _post_attn(x, o, w_o, mix_layer, g, w_gate, w_up, w_down, layer, g_final, tm, final):
    m, d = x.shape
    d_o = o.shape[1]
    d_ff = w_gate.shape[2]
    chunk = 256
    assert m % tm == 0 and d_ff % chunk == 0
    kern = functools.partial(_post_kernel, chunk=chunk, final=final)

    def resident(shape, slot=None):
        if slot is None:
            return pl.BlockSpec(shape, lambda i: (0, 0), pipeline_mode=pl.Buffered(1))
        return pl.BlockSpec((None,) + shape, lambda i: (slot, 0, 0), pipeline_mode=pl.Buffered(1))

    return pl.pallas_call(
        kern,
        grid=(m // tm,),
        in_specs=[pl.BlockSpec((tm, d), lambda i: (i, 0)),
                  pl.BlockSpec((tm, d_o), lambda i: (i, 0)),
                  resident((d_o, d), mix_layer),
                  resident((1, d)),
                  resident((d, d_ff), layer),
                  resident((d, d_ff), layer),
                  resident((d_ff, d), layer),
                  resident((1, d))],
        out_specs=pl.BlockSpec((tm, d), lambda i: (i, 0)),
        out_shape=jax.ShapeDtypeStruct((m, d), F32),
        scratch_shapes=[pltpu.VMEM((tm, d), BF16), pltpu.VMEM((tm, d), F32)],
        compiler_params=_cparams(("arbitrary",)),
        name="post_attn",
    )(x, o, w_o, g, w_gate, w_up, w_down, g_final)


def _diff_lambda(lam_ref, lam_init):
    lp = lam_ref[...]
    s1 = jnp.sum(lp[0:1] * lp[1:2], axis=-1, keepdims=True)
    s2 = jnp.sum(lp[2:3] * lp[3:4], axis=-1, keepdims=True)
    return jnp.exp(s1) - jnp.exp(s2) + lam_init


def _near_bias(s, bias_ref, mp, n_g, diag):
    parts = []
    for g in range(n_g):
        tile = bias_ref[g * 2 + mp]
        t_prev, t_own = tile[:, :BLK], tile[:, BLK:]
        top = s[g * TQ:g * TQ + BLK]
        bot = s[g * TQ + BLK:(g + 1) * TQ]
        if diag:
            top = jnp.concatenate([top[:, :BLK] + t_own, jnp.full((BLK, BLK), NEG_INF, F32)], axis=1)
            bot = jnp.concatenate([bot[:, :BLK] + t_prev, bot[:, BLK:] + t_own], axis=1)
        else:
            top = jnp.concatenate([top[:, :BLK], top[:, BLK:] + t_prev], axis=1)
        parts += [top, bot]
    return jnp.concatenate(parts, axis=0)


def _diff_prompt_block(n_chunks, q_ref, k_ref, v_ref, bias_ref, lam_ref, subln_ref, o_ref, s_scr, m_scr, v1_scr,
                       *, lam_init, n_g):
    dv = v_ref.shape[1]
    if n_chunks == 1:
        v1_scr[:, :dv] = v_ref[...]
        v1_scr[:, dv:] = jnp.ones((v1_scr.shape[0], v1_scr.shape[1] - dv), v1_scr.dtype)
    lam = _diff_lambda(lam_ref, lam_init)
    o_maps = []
    for mp in range(2):
        q = jnp.concatenate([q_ref[:, (g * 2 + mp) * HEAD_DIM:(g * 2 + mp + 1) * HEAD_DIM]
                             for g in range(n_g)], axis=0)
        ksl = slice(mp * HEAD_DIM, (mp + 1) * HEAD_DIM)

        mx = None
        for c in range(n_chunks):
            s = lax.dot_general(q, k_ref[c * TQ:(c + 1) * TQ, ksl], _NT, preferred_element_type=F32)
            if c >= n_chunks - 2:
                s = _near_bias(s, bias_ref, mp, n_g, diag=(c == n_chunks - 1))
            s_scr[:, c * TQ:(c + 1) * TQ] = s
            cm = jnp.maximum(s[:, :BLK], s[:, BLK:])
            mx = cm if mx is None else jnp.maximum(mx, cm)
        m_scr[...] = jnp.broadcast_to(jnp.max(mx, axis=1, keepdims=True), m_scr.shape)

        p = jnp.concatenate([jnp.exp2(s_scr[:, j * BLK:(j + 1) * BLK] - m_scr[...]).astype(BF16)
                             for j in range(n_chunks * TQ // BLK)], axis=1)
        acc = jnp.dot(p, v1_scr[0:n_chunks * TQ, :], preferred_element_type=F32)
        o_maps.append(acc[:, :dv] / acc[:, dv:dv + 1])

    d = o_maps[0] - lam * o_maps[1]
    d = d * lax.rsqrt(jnp.mean(d * d, axis=-1, keepdims=True) + RMS_EPS) * subln_ref[...]
    d = d * (1.0 - lam_init)
    for g in range(n_g):
        o_ref[:, g * dv:(g + 1) * dv] = d[g * TQ:(g + 1) * TQ].astype(o_ref.dtype)


def _head_of_row(shape, n_heads, n_kv):
    return (lax.broadcasted_iota(jnp.int32, shape, 0) % n_heads) // (n_heads // n_kv)


def _diff_sample_init(m_scr, l_scr, acc_scr):
    m_scr[...] = jnp.full(m_scr.shape, NEG_INF, F32)
    l_scr[...] = jnp.zeros(l_scr.shape, F32)
    acc_scr[...] = jnp.zeros(acc_scr.shape, F32)


def _diff_sample_update(grp, last, q_ref, bias_ref, k_refs, v_refs, m_scr, l_scr, acc_scr, *, n_kv):
    pages = len(k_refs)
    n_rows, dv = q_ref.shape
    n_heads = n_rows // 2
    prows = k_refs[0].shape[0]
    qb = (q_ref[...].astype(F32) * 0.125).astype(BF16)
    own = (_head_of_row((n_rows, prows), n_heads, n_kv)
           == lax.broadcasted_iota(jnp.int32, (n_rows, prows), 1) % n_kv)
    near_bias = jnp.where(grp == last, bias_ref[:, 0:prows], 0.0)
    s_list = []
    for p in range(pages):
        s = lax.dot_general(qb, k_refs[p][...].astype(BF16), _NT, preferred_element_type=F32)
        if p == pages - 1:
            s = s + near_bias
        s_list.append(jnp.where(own, s, NEG_INF))
    s_all = jnp.concatenate(s_list, axis=1)
    m_old = m_scr[...]
    m_new = jnp.maximum(m_old, jnp.max(s_all, axis=1, keepdims=True))
    alpha = jnp.exp(m_old - m_new)
    p_all = jnp.exp(s_all - m_new)
    l_new = alpha * l_scr[...] + jnp.sum(p_all, axis=1, keepdims=True)
    acc = alpha * acc_scr[...]
    for p in range(pages):
        acc = acc + jnp.dot(p_all[:, p * prows:(p + 1) * prows].astype(BF16), v_refs[p][...].astype(BF16),
                            preferred_element_type=F32)
    m_scr[...] = m_new
    l_scr[...] = l_new
    acc_scr[...] = acc


def _diff_sample_finish(q_ref, knew_ref, vnew_ref, bias_ref, lam_ref, subln_ref, o_ref, m_scr, l_scr, acc_scr, *,
                        lam_init, n_kv):
    n_rows, dv = q_ref.shape
    n_heads = n_rows // 2
    self_col = bias_ref.shape[1] - BLK
    qf = q_ref[...].astype(F32) * 0.125
    row_kv = _head_of_row((n_rows, dv), n_heads, n_kv)
    k_self = jnp.zeros((n_rows, dv), F32)
    v_self = jnp.zeros((n_rows, dv), F32)
    for kk in range(n_kv):
        k_self = k_self + jnp.where(row_kv == kk, knew_ref[:, kk * dv:(kk + 1) * dv], 0.0)
        v_self = v_self + jnp.where(row_kv == kk, vnew_ref[:, kk * dv:(kk + 1) * dv], 0.0)
    s_self = jnp.sum(qf * k_self, axis=1, keepdims=True) + bias_ref[:, self_col:self_col + 1]
    m_old = m_scr[...]
    m_f = jnp.maximum(m_old, s_self)
    a = jnp.exp(m_old - m_f)
    p_self = jnp.exp(s_self - m_f)
    l_f = a * l_scr[...] + p_self
    o = (a * acc_scr[...] + p_self * v_self) / l_f
    lam = _diff_lambda(lam_ref, lam_init)
    d = o[:n_heads] - lam * o[n_heads:]
    d = d * lax.rsqrt(jnp.mean(d * d, axis=-1, keepdims=True) + RMS_EPS) * subln_ref[...]
    o_ref[...] = (d * (1.0 - lam_init)).astype(o_ref.dtype)


def _page_copies(pt_ref, ck_ref, cv_ref, kbuf, vbuf, sem, step, slot, li):
    pages = kbuf.shape[1]
    copies = []
    for p in range(pages):
        phys = pt_ref[step * pages + p]
        copies.append(pltpu.make_async_copy(ck_ref.at[phys, li], kbuf.at[slot, p], sem.at[slot, 0]))
        copies.append(pltpu.make_async_copy(cv_ref.at[phys, li], vbuf.at[slot, p], sem.at[slot, 1]))
    return copies


def _diff_attn_kernel(pt_ref, q_ref, k_ref, v_ref, bias_ref, lam_ref, subln_ref,
                      qs_ref, knew_ref, vnew_ref, bias_rows_ref, ck_ref, cv_ref,
                      o_ref, os_ref, s_scr, m_scr, v1_scr, sm_scr, sl_scr, sacc_scr, kbuf, vbuf, sem, *,
                      li, lam_init, n_g, n_kv, n_qblk, n_groups):
    step = pl.program_id(0)
    n_steps = pl.num_programs(0)
    slot = step % 2
    pages = kbuf.shape[1]
    copies = functools.partial(_page_copies, pt_ref, ck_ref, cv_ref, kbuf, vbuf, sem, li=li)

    @pl.when(step == 0)
    def _():
        for c in copies(step, slot):
            c.start()

    for c in copies(step, slot):
        c.wait()

    @pl.when(step + 1 < n_steps)
    def _():
        for c in copies(step + 1, 1 - slot):
            c.start()

    k_refs = [kbuf.at[slot, p] for p in range(pages)]
    v_refs = [vbuf.at[slot, p] for p in range(pages)]
    grp = step % n_groups
    last = n_groups - 1
    sample_state = (sm_scr, sl_scr, sacc_scr)
    pl.when(grp == 0)(functools.partial(_diff_sample_init, *sample_state))

    def block(n_chunks):
        _diff_sample_update(grp, last, qs_ref, bias_rows_ref, k_refs, v_refs, *sample_state, n_kv=n_kv)
        _diff_prompt_block(n_chunks, q_ref, k_ref, v_ref, bias_ref, lam_ref, subln_ref, o_ref, s_scr, m_scr,
                           v1_scr, lam_init=lam_init, n_g=n_g)

    for blk in range(n_qblk):
        pl.when(step % n_qblk == blk)(functools.partial(block, blk + 1))
    pl.when(grp == last)(functools.partial(
        _diff_sample_finish, qs_ref, knew_ref, vnew_ref, bias_rows_ref, lam_ref, subln_ref, os_ref, *sample_state,
        lam_init=lam_init, n_kv=n_kv))


def _diff_attn(q, kv, bias_causal, q_rows, k_new, v_new, bias_rows, cache_k, cache_v, li, page_table,
               lam_p, subln, lam_init, batch, seq, n_g):
    dec_b, n_rows, dv = q_rows.shape
    n_pool, n_layers, page, n_kv, _ = cache_k.shape
    n_pages = page_table.shape[1]
    n_heads = n_rows // 2
    width = n_kv * dv
    prows = page * n_kv
    qw = n_g * 2 * HEAD_DIM
    n_qblk = seq // TQ
    steps = batch * n_kv * n_qblk
    n_groups = steps // dec_b
    pages = n_pages // n_groups
    assert seq % TQ == 0 and steps % dec_b == 0 and n_pages % n_groups == 0
    q3 = q.reshape(batch, seq, q.shape[1])
    kv3 = kv.reshape(batch, seq, kv.shape[1])
    cache_k = cache_k.reshape(n_pool, n_layers, prows, dv)
    cache_v = cache_v.reshape(n_pool, n_layers, prows, dv)
    pt_flat = page_table.reshape(-1)

    def prompt_map(f):
        return lambda t, pt: f(t // (n_kv * n_qblk), (t // n_qblk) % n_kv, t % n_qblk)

    def seq_map(t, pt):
        return (t // n_groups, 0, 0)

    kern = functools.partial(_diff_attn_kernel, li=li, lam_init=lam_init, n_g=n_g, n_kv=n_kv,
                             n_qblk=n_qblk, n_groups=n_groups)
    grid_spec = pltpu.PrefetchScalarGridSpec(
        num_scalar_prefetch=1,
        grid=(steps,),
        in_specs=[pl.BlockSpec((None, TQ, qw), prompt_map(lambda b, k, i: (b, i, k))),
                  pl.BlockSpec((None, seq, dv), prompt_map(lambda b, k, i: (b, 0, k))),
                  pl.BlockSpec((None, seq, dv), prompt_map(lambda b, k, i: (b, 0, n_kv + k))),
                  pl.BlockSpec((n_g * 2, BLK, 2 * BLK), prompt_map(lambda b, k, i: (k, 0, 0))),
                  pl.BlockSpec((4, HEAD_DIM), lambda t, pt: (0, 0)),
                  pl.BlockSpec((1, dv), lambda t, pt: (0, 0)),
                  pl.BlockSpec((None, n_rows, dv), seq_map),
                  pl.BlockSpec((None, 1, width), seq_map),
                  pl.BlockSpec((None, 1, width), seq_map),
                  pl.BlockSpec(bias_rows.shape, lambda t, pt: (0, 0)),
                  pl.BlockSpec(memory_space=pl.ANY),
                  pl.BlockSpec(memory_space=pl.ANY)],
        out_specs=[pl.BlockSpec((None, TQ, n_g * dv), prompt_map(lambda b, k, i: (b, i, k))),
                   pl.BlockSpec((None, n_heads, dv), seq_map)],
        scratch_shapes=[pltpu.VMEM((n_g * TQ, seq), F32), pltpu.VMEM((n_g * TQ, BLK), F32),
                        pltpu.VMEM((seq, 2 * dv), BF16),
                        pltpu.VMEM((n_rows, 1), F32), pltpu.VMEM((n_rows, 1), F32),
                        pltpu.VMEM((n_rows, dv), F32),
                        pltpu.VMEM((2, pages, prows, dv), F32), pltpu.VMEM((2, pages, prows, dv), F32),
                        pltpu.SemaphoreType.DMA((2, 2))],
    )
    o, o_s = pl.pallas_call(
        kern,
        grid_spec=grid_spec,
        out_shape=[jax.ShapeDtypeStruct((batch, seq, n_kv * n_g * dv), BF16),
                   jax.ShapeDtypeStruct((dec_b, n_heads, dv), BF16)],
        compiler_params=_cparams(("arbitrary",)),
        name="diff_attn",
    )(pt_flat, q3, kv3, kv3, bias_causal, lam_p, subln,
      q_rows, k_new.reshape(dec_b, 1, width), v_new.reshape(dec_b, 1, width), bias_rows, cache_k, cache_v)
    return o.reshape(batch * seq, n_kv * n_g * dv), o_s.reshape(dec_b, n_heads * dv)


def _swa_prompt_kernel(sink_ref, q_ref, kvp_ref, kvc_ref, bias_ref, o_ref, *, n_kv, n_g):
    i = pl.program_id(1)
    col = lax.broadcasted_iota(jnp.int32, (BLK, 2 * BLK), 1)
    has_prev = (col >= BLK) | (i > 0)
    v_off = n_kv * HEAD_DIM
    outs = []
    for kv in range(n_kv):
        ksl = slice(kv * HEAD_DIM, (kv + 1) * HEAD_DIM)
        vsl = slice(v_off + kv * HEAD_DIM, v_off + (kv + 1) * HEAD_DIM)
        kk = jnp.concatenate([kvp_ref[:, ksl], kvc_ref[:, ksl]], axis=0)
        vv = jnp.concatenate([kvp_ref[:, vsl], kvc_ref[:, vsl]], axis=0)
        for g in range(n_g):
            h = kv * n_g + g
            q = q_ref[:, h * HEAD_DIM:(h + 1) * HEAD_DIM] * 0.125
            s = lax.dot_general(q, kk, _NT, preferred_element_type=F32) + bias_ref[h]
            s = jnp.where(has_prev, s, NEG_INF)
            sink = sink_ref[0, h]
            m = jnp.maximum(jnp.max(s, axis=1, keepdims=True), sink)
            p = jnp.exp(s - m)
            denom = jnp.sum(p, axis=1, keepdims=True) + jnp.exp(sink - m)
            o = jnp.dot(p.astype(BF16), vv, preferred_element_type=F32) / denom
            outs.append(o)
    o_ref[...] = jnp.concatenate(outs, axis=1).astype(o_ref.dtype)


def _swa_attn_prompt(q, kv, bias_band, sinks, batch, seq, n_kv, n_g):
    qw = q.shape[1]
    kvw = kv.shape[1]
    q3 = q.reshape(batch, seq, qw)
    kv3 = kv.reshape(batch, seq, kvw)
    kern = functools.partial(_swa_prompt_kernel, n_kv=n_kv, n_g=n_g)
    o = pl.pallas_call(
        kern,
        grid=(batch, seq // BLK),
        in_specs=[pl.BlockSpec(memory_space=pltpu.SMEM),
                  pl.BlockSpec((None, BLK, qw), lambda b, i: (b, i, 0)),
                  pl.BlockSpec((None, BLK, kvw), lambda b, i: (b, jnp.maximum(i - 1, 0), 0)),
                  pl.BlockSpec((None, BLK, kvw), lambda b, i: (b, i, 0)),
                  pl.BlockSpec((n_kv * n_g, BLK, 2 * BLK), lambda b, i: (0, 0, 0))],
        out_specs=pl.BlockSpec((None, BLK, qw), lambda b, i: (b, i, 0)),
        out_shape=jax.ShapeDtypeStruct((batch, seq, qw), BF16),
        compiler_params=_cparams(("arbitrary", "arbitrary")),
        name="swa_attn_prompt",
    )(sinks.reshape(1, -1), q3, kv3, kv3, bias_band)
    return o.reshape(batch * seq, qw)


def _swa_sample_kernel(q_ref, kvnew_ref, kcol_ref, sk_ref, sv_ref, bias_ref, sink_ref, o_ref, nk_ref, nv_ref,
                       *, n_kv, bb):
    n_heads = q_ref.shape[1]
    n_g = n_heads // n_kv
    buf = sk_ref.shape[-1]
    newest = lax.broadcasted_iota(jnp.int32, (HEAD_DIM, buf), 1) == buf - 1
    for s in range(bb):
        for kv in range(n_kv):
            hs = slice(kv * n_g, (kv + 1) * n_g)
            qf = q_ref[s, hs, :].astype(F32) * 0.125
            k_row = kvnew_ref[s, :, kv * HEAD_DIM:(kv + 1) * HEAD_DIM]
            v_row = kvnew_ref[s, :, (n_kv + kv) * HEAD_DIM:(n_kv + kv + 1) * HEAD_DIM]
            st_k = sk_ref[s, kv]
            st_v = sv_ref[s, kv]
            sink = sink_ref[hs, :]
            sc = jnp.dot(qf.astype(BF16), st_k.astype(BF16), preferred_element_type=F32) + bias_ref[hs, 0:BLK]
            s_self = jnp.sum(qf * k_row, axis=1, keepdims=True) + bias_ref[hs, BLK:BLK + 1]
            m = jnp.maximum(jnp.maximum(jnp.max(sc, axis=1, keepdims=True), s_self), sink)
            p = jnp.exp(sc - m)
            p_self = jnp.exp(s_self - m)
            denom = jnp.sum(p, axis=1, keepdims=True) + p_self + jnp.exp(sink - m)
            o = lax.dot_general(p.astype(BF16), st_v.astype(BF16), _NT, preferred_element_type=F32)
            o_ref[s, hs, :] = ((o + p_self * v_row) / denom).astype(o_ref.dtype)
            nk_ref[s, kv] = jnp.where(newest, kcol_ref[s, kv], pltpu.roll(st_k, buf - 1, axis=1))
            nv_ref[s, kv] = jnp.where(newest, kcol_ref[s, n_kv + kv], pltpu.roll(st_v, buf - 1, axis=1))


def _swa_attn_sample(q, kv_new, state_k_t, state_v_t, li, bias_rows, sinks, bb):
    b, n_heads, _ = q.shape
    n_kv, buf = state_k_t.shape[2], state_k_t.shape[4]
    kw = n_kv * HEAD_DIM
    assert buf == WINDOW and b % bb == 0
    kern = functools.partial(_swa_sample_kernel, n_kv=n_kv, bb=bb)
    st_spec = pl.BlockSpec((bb, None, n_kv, HEAD_DIM, buf), lambda i: (i, li, 0, 0, 0))
    new_spec = pl.BlockSpec((bb, n_kv, HEAD_DIM, buf), lambda i: (i, 0, 0, 0))
    new_shape = jax.ShapeDtypeStruct((b, n_kv, HEAD_DIM, buf), F32)
    o, nk, nv = pl.pallas_call(
        kern,
        grid=(b // bb,),
        in_specs=[pl.BlockSpec((bb, n_heads, HEAD_DIM), lambda i: (i, 0, 0)),
                  pl.BlockSpec((bb, 1, 2 * kw), lambda i: (i, 0, 0)),
                  pl.BlockSpec((bb, 2 * n_kv, HEAD_DIM, 1), lambda i: (i, 0, 0, 0)),
                  st_spec, st_spec,
                  pl.BlockSpec((n_heads, 2 * BLK), lambda i: (0, 0)),
                  pl.BlockSpec((n_heads, 1), lambda i: (0, 0))],
        out_specs=[pl.BlockSpec((bb, n_heads, HEAD_DIM), lambda i: (i, 0, 0)), new_spec, new_spec],
        out_shape=[jax.ShapeDtypeStruct((b, n_heads, HEAD_DIM), F32), new_shape, new_shape],
        compiler_params=_cparams(("arbitrary",)),
        name="swa_attn_sample",
    )(q, kv_new.reshape(b, 1, 2 * kw), kv_new.reshape(b, 2 * n_kv, HEAD_DIM, 1), state_k_t, state_v_t,
      bias_rows, sinks.reshape(-1, 1))
    return o.reshape(b, n_heads * HEAD_DIM).astype(BF16), nk, nv


def _block_rows(q_rows, n_kv_blocks, block_of_row):
    w = q_rows.shape[-1]
    tiled = jnp.tile(q_rows, (1, 1, n_kv_blocks))
    lane_block = np.arange(n_kv_blocks * w)[None, :] // w
    mask = jnp.asarray(lane_block == np.asarray(block_of_row)[:, None])
    return jnp.where(mask[None], tiled, jnp.zeros_like(tiled))


def kernel(x_prompt, x_sample, cache_k_diff, cache_v_diff, state_k_swa, state_v_swa, page_table,
           rel_bias, g_mix, g_ffn, g_final, w_qkv_diff, w_o_diff, lambda_diff, subln_diff,
           w_qkv_swa, b_qkv_swa, w_o_swa, sinks_swa, w_gate, w_up, w_down):
    batch, seq, d_model = x_prompt.shape
    dec_b = x_sample.shape[0]
    depth = g_mix.shape[0]
    n_pool, n_diff_layers, page, n_kv_diff, kd = cache_k_diff.shape
    n_heads_diff = w_o_diff.shape[1] // kd
    g_diff = n_heads_diff // n_kv_diff
    n_kv_swa = state_k_swa.shape[3]
    n_heads_swa = sinks_swa.shape[1]
    g_swa = n_heads_swa // n_kv_swa
    buf = state_k_swa.shape[2]
    q_diff = n_heads_diff * 2 * HEAD_DIM
    q_swa = n_heads_swa * HEAD_DIM
    kw_diff = n_kv_diff * kd
    kw_swa = n_kv_swa * HEAD_DIM

    bias_causal, bias_causal2, bias_band = _bias_tiles(rel_bias)
    diff_cols = np.array([2 * (r % n_heads_diff) + r // n_heads_diff for r in range(2 * n_heads_diff)])
    diff_row_map = np.arange(2 * n_heads_diff) // n_heads_diff
    near = bias_causal[:, 0, :][diff_cols]
    bias_rows_diff = jnp.concatenate([jnp.repeat(near[:, :BLK], n_kv_diff, axis=1), near[:, BLK:]], axis=1)
    bias_rows_swa = bias_band[:, 0, :]

    st_k = state_k_swa.transpose(0, 1, 3, 4, 2)
    st_v = state_v_swa.transpose(0, 1, 3, 4, 2)

    xp = x_prompt.reshape(batch * seq, d_model)
    xs = x_sample.reshape(dec_b, d_model)
    tm_p, tm_s = 512, dec_b
    zero_bias = jnp.zeros((1, w_qkv_diff.shape[2]), F32)
    wq_diff, wo_diff = w_qkv_diff.astype(BF16), w_o_diff.astype(BF16)
    wq_swa, wo_swa = w_qkv_swa.astype(BF16), w_o_swa.astype(BF16)
    wg, wu, wd = w_gate.astype(BF16), w_up.astype(BF16), w_down.astype(BF16)
    kds, vds, ksp, vsp, kss, vss = [], [], [], [], [], []
    new_cache = None

    for i in range(depth):
        li = i // 2
        g_m = g_mix[i].reshape(1, d_model)
        if i % 2 == 0:
            lam_init = 0.8 - 0.6 * math.exp(-0.3 * i)
            w_o = wo_diff
            subln = subln_diff[li].reshape(1, kd)
            qp, kvp_b, k_cache, v_cache = _qkv_proj_paged(xp, g_m, wq_diff, q_diff, n_kv_diff, tm_p, batch, li,
                                                          new_cache)
            new_cache = (k_cache, v_cache)
            qs, kvs_f, _ = _qkv_proj(xs, g_m, wq_diff, li, zero_bias, q_diff, tm_s)
            q_rows = qs.reshape(dec_b, n_kv_diff, g_diff, 2, HEAD_DIM).transpose(0, 3, 1, 2, 4)
            q_rows = _block_rows(q_rows.reshape(dec_b, 2 * n_heads_diff, HEAD_DIM), 2, diff_row_map)
            op, os_ = _diff_attn(qp, kvp_b, bias_causal2, q_rows, kvs_f[:, :kw_diff], kvs_f[:, kw_diff:],
                                 bias_rows_diff, cache_k_diff, cache_v_diff, li, page_table,
                                 lambda_diff[li], subln, lam_init, batch, seq, g_diff)
            kds.append(kvs_f[:, :kw_diff].reshape(dec_b, 1, n_kv_diff, kd))
            vds.append(kvs_f[:, kw_diff:].reshape(dec_b, 1, n_kv_diff, kd))
        else:
            w_o = wo_swa
            b_qkv = b_qkv_swa[li].reshape(1, -1)
            qp, kvp_f, kvp_b = _qkv_proj(xp, g_m, wq_swa, li, b_qkv, q_swa, tm_p)
            qs, kvs_f, _ = _qkv_proj(xs, g_m, wq_swa, li, b_qkv, q_swa, tm_s)
            op = _swa_attn_prompt(qp, kvp_b, bias_band, sinks_swa[li], batch, seq, n_kv_swa, g_swa)
            os_, nk, nv = _swa_attn_sample(qs.reshape(dec_b, n_heads_swa, HEAD_DIM).astype(F32), kvs_f,
                                           st_k, st_v, li, bias_rows_swa, sinks_swa[li], bb=8)
            tail = kvp_f.reshape(batch, seq, 2 * kw_swa)[:, seq - min(WINDOW, seq):]
            ksp.append(tail[..., :kw_swa].reshape(batch, -1, n_kv_swa, HEAD_DIM))
            vsp.append(tail[..., kw_swa:].reshape(batch, -1, n_kv_swa, HEAD_DIM))
            kss.append(nk.transpose(0, 3, 1, 2))
            vss.append(nv.transpose(0, 3, 1, 2))
        final = i == depth - 1
        g_f = g_ffn[i].reshape(1, d_model)
        gfin = g_final.reshape(1, d_model)
        xp = _post_attn(xp, op, w_o, li, g_f, wg, wu, wd, i, gfin, tm_p, final)
        xs = _post_attn(xs, os_, w_o, li, g_f, wg, wu, wd, i, gfin, tm_s, final)

    k_cache, v_cache = new_cache
    cache_shape = (batch, n_diff_layers, seq, n_kv_diff, kd)
    return (xp.reshape(batch, seq, d_model), xs.reshape(dec_b, 1, d_model),
            k_cache.reshape(cache_shape), v_cache.reshape(cache_shape),
            jnp.stack(kds, axis=1), jnp.stack(vds, axis=1),
            jnp.stack(ksp, axis=1), jnp.stack(vsp, axis=1), jnp.stack(kss, axis=1), jnp.stack(vss, axis=1))
```

```python
import functools
import math

import numpy as np
import jax
import jax.numpy as jnp
from jax import lax
from jax.experimental import pallas as pl
from jax.experimental.pallas import tpu as pltpu

F32 = jnp.float32
BF16 = jnp.bfloat16

HEAD_DIM = 64
WINDOW = 128
NUM_BUCKETS = 32
MAX_DISTANCE = 128
RMS_EPS = 1e-6
NEG_INF = -1e30
LOG2E = math.log2(math.e)
BLK = 128
TQ = 2 * BLK
LANES = 128
VMEM_LIMIT = 56 * 1024 * 1024

_NT = (((1,), (1,)), ((), ()))


def _rms(x, g):
    return x * lax.rsqrt(jnp.mean(x * x, axis=-1, keepdims=True) + RMS_EPS) * g


def _cparams(sem):
    return pltpu.CompilerParams(dimension_semantics=sem, vmem_limit_bytes=VMEM_LIMIT)


def _bucket_table():
    dist = np.arange(BLK)[:, None] + BLK - np.arange(2 * BLK)[None, :]
    n = np.maximum(dist, 0)
    max_exact = NUM_BUCKETS // 2
    nf = np.maximum(n, 1).astype(np.float32)
    large = max_exact + (np.log(nf / np.float32(max_exact)) / np.float32(math.log(MAX_DISTANCE / max_exact))
                         * np.float32(NUM_BUCKETS - max_exact)).astype(np.int32)
    large = np.minimum(large, NUM_BUCKETS - 1)
    return np.where(n < max_exact, n, large).astype(np.int32)


def _bias_kernel(table_ref, bucket_ref, causal_ref, causal2_ref, band_ref):
    col = pl.program_id(0)
    bucket = bucket_ref[...]
    t = jnp.zeros(bucket.shape, F32)
    for b in range(NUM_BUCKETS):
        t = jnp.where(bucket == b, table_ref[b, col], t)
    far = table_ref[NUM_BUCKETS - 1, col]
    dist = (lax.broadcasted_iota(jnp.int32, bucket.shape, 0) + BLK
            - lax.broadcasted_iota(jnp.int32, bucket.shape, 1))
    causal_ref[...] = jnp.where(dist >= 0, t - far, NEG_INF)
    causal2_ref[...] = jnp.where(dist >= 0, (t - far) * LOG2E, NEG_INF)
    band_ref[...] = jnp.where((dist >= 0) & (dist < WINDOW), t, NEG_INF)


def _bias_tiles(rel_bias):
    n_cols = rel_bias.shape[1]
    bucket = jnp.asarray(_bucket_table())
    shape = jax.ShapeDtypeStruct((n_cols, BLK, 2 * BLK), F32)
    return pl.pallas_call(
        _bias_kernel,
        grid=(n_cols,),
        in_specs=[pl.BlockSpec(memory_space=pltpu.SMEM),
                  pl.BlockSpec((BLK, 2 * BLK), lambda c: (0, 0))],
        out_specs=[pl.BlockSpec((None, BLK, 2 * BLK), lambda c: (c, 0, 0))] * 3,
        out_shape=[shape] * 3,
        compiler_params=_cparams(("arbitrary",)),
        name="bias_tiles",
    )(rel_bias, bucket)


def _qkv_kernel(x_ref, g_ref, w_ref, b_ref, q_ref, kvf_ref, kvb_ref, h_scr, *, n_q, chunk):
    h_scr[...] = _rms(x_ref[...], g_ref[...]).astype(BF16)
    n_total = w_ref.shape[1]
    for c0 in range(0, n_total, chunk):
        c1 = min(c0 + chunk, n_total)
        acc = jnp.dot(h_scr[...], w_ref[:, c0:c1], preferred_element_type=F32) + b_ref[:, c0:c1]
        if c1 <= n_q:
            q_ref[:, c0:c1] = acc.astype(BF16)
        else:
            kvf_ref[:, c0 - n_q:c1 - n_q] = acc
            kvb_ref[:, c0 - n_q:c1 - n_q] = acc.astype(BF16)


def _qkv_proj(x, g, w, layer, b, n_q, tm):
    m, d = x.shape
    n = w.shape[2]
    n_kv = n - n_q
    chunk = 256
    assert m % tm == 0 and n_q % chunk == 0 and n_kv % chunk == 0
    kern = functools.partial(_qkv_kernel, n_q=n_q, chunk=chunk)
    return pl.pallas_call(
        kern,
        grid=(m // tm,),
        in_specs=[pl.BlockSpec((tm, d), lambda i: (i, 0)),
                  pl.BlockSpec((1, d), lambda i: (0, 0)),
                  pl.BlockSpec((None, d, n), lambda i: (layer, 0, 0)),
                  pl.BlockSpec((1, n), lambda i: (0, 0))],
        out_specs=[pl.BlockSpec((tm, n_q), lambda i: (i, 0)),
                   pl.BlockSpec((tm, n_kv), lambda i: (i, 0)),
                   pl.BlockSpec((tm, n_kv), lambda i: (i, 0))],
        out_shape=[jax.ShapeDtypeStruct((m, n_q), BF16),
                   jax.ShapeDtypeStruct((m, n_kv), F32),
                   jax.ShapeDtypeStruct((m, n_kv), BF16)],
        scratch_shapes=[pltpu.VMEM((tm, d), BF16)],
        compiler_params=_cparams(("arbitrary",)),
        name="qkv_proj",
    )(x, g, w, b)


def _qkv_paged_kernel(x_ref, g_ref, w_ref, *rest, n_q, n_kv, chunk, first):
    q_ref, kvb_ref, ko_ref, vo_ref, h_scr = rest if first else rest[2:]
    tm = x_ref.shape[0]
    dv = ko_ref.shape[-1]
    h_scr[...] = _rms(x_ref[...], g_ref[...]).astype(BF16)
    n_total = w_ref.shape[1]
    for c0 in range(0, n_total, chunk):
        acc = jnp.dot(h_scr[...], w_ref[:, c0:c0 + chunk], preferred_element_type=F32)
        if c0 < n_q:
            q_ref[:, c0:c0 + chunk] = (acc * (HEAD_DIM ** -0.5 * LOG2E)).astype(BF16)
            continue
        kvb_ref[:, c0 - n_q:c0 - n_q + chunk] = acc.astype(BF16)
        for j in range(chunk // dv):
            head = (c0 - n_q) // dv + j
            dst = ko_ref if head < n_kv else vo_ref
            rows = pl.ds(head % n_kv, tm, stride=n_kv)
            val = acc[:, j * dv:(j + 1) * dv]
            if first:
                for layer in range(dst.shape[0]):
                    dst[layer, rows, :] = val
            else:
                dst[rows, :] = val


def _qkv_proj_paged(x, g, w, n_q, n_kv, tm, batch, layer, prev):
    m, d = x.shape
    n_layers, _, n = w.shape
    seq = m // batch
    dv = (n - n_q) // (2 * n_kv)
    chunk = 256
    first = prev is None
    assert seq % tm == 0 and n_q % chunk == 0 and (n - n_q) % chunk == 0 and chunk % dv == 0
    tiles = seq // tm
    kern = functools.partial(_qkv_paged_kernel, n_q=n_q, n_kv=n_kv, chunk=chunk, first=first)
    cache_shape = jax.ShapeDtypeStruct((batch, n_layers, seq * n_kv, dv), F32)
    if first:
        cache_spec = pl.BlockSpec((None, n_layers, tm * n_kv, dv), lambda i: (i // tiles, 0, i % tiles, 0))
        extra_in, extra_args, aliases = [], [], {}
    else:
        cache_spec = pl.BlockSpec((None, None, tm * n_kv, dv), lambda i: (i // tiles, layer, i % tiles, 0))
        extra_in = [pl.BlockSpec(memory_space=pl.ANY)] * 2
        extra_args, aliases = list(prev), {3: 2, 4: 3}
    return pl.pallas_call(
        kern,
        grid=(m // tm,),
        in_specs=[pl.BlockSpec((tm, d), lambda i: (i, 0)),
                  pl.BlockSpec((1, d), lambda i: (0, 0)),
                  pl.BlockSpec((None, d, n), lambda i: (layer, 0, 0))] + extra_in,
        out_specs=[pl.BlockSpec((tm, n_q), lambda i: (i, 0)),
                   pl.BlockSpec((tm, n - n_q), lambda i: (i, 0)),
                   cache_spec, cache_spec],
        out_shape=[jax.ShapeDtypeStruct((m, n_q), BF16),
                   jax.ShapeDtypeStruct((m, n - n_q), BF16),
                   cache_shape, cache_shape],
        scratch_shapes=[pltpu.VMEM((tm, d), BF16)],
        input_output_aliases=aliases,
        compiler_params=_cparams(("arbitrary",)),
        name="qkv_proj_paged",
    )(x, g, w, *extra_args)


def _post_kernel(x_ref, o_ref, wo_ref, g_ref, wg_ref, wu_ref, wd_ref, gf_ref, out_ref,
                 h_scr, acc_scr, *, chunk, final):
    x1 = x_ref[...] + jnp.dot(o_ref[...], wo_ref[...], preferred_element_type=F32)
    acc_scr[...] = x1
    h_scr[...] = _rms(x1, g_ref[...]).astype(BF16)
    d_ff = wg_ref.shape[1]
    for c0 in range(0, d_ff, chunk):
        gate = jnp.dot(h_scr[...], wg_ref[:, c0:c0 + chunk], preferred_element_type=F32)
        up = jnp.dot(h_scr[...], wu_ref[:, c0:c0 + chunk], preferred_element_type=F32)
        act = (gate * (1.0 / (1.0 + jnp.exp(-gate))) * up).astype(BF16)
        acc_scr[...] += jnp.dot(act, wd_ref[c0:c0 + chunk, :], preferred_element_type=F32)
    if final:
        out_ref[...] = _rms(acc_scr[...], gf_ref[...])
    else:
        out_ref[...] = acc_scr[...]


def _post_attn(x, o, w_o, mix_layer, g, w_gate, w_up, w_down, layer, g_final, tm, final):
    m, d = x.shape
    d_o = o.shape[1]
    d_ff = w_gate.shape[2]
    chunk = 256
    assert m % tm == 0 and d_ff % chunk == 0
    kern = functools.partial(_post_kernel, chunk=chunk, final=final)

    def resident(shape, slot=None):
        if slot is None:
            return pl.BlockSpec(shape, lambda i: (0, 0), pipeline_mode=pl.Buffered(1))
        return pl.BlockSpec((None,) + shape, lambda i: (slot, 0, 0), pipeline_mode=pl.Buffered(1))

    return pl.pallas_call(
        kern,
        grid=(m // tm,),
        in_specs=[pl.BlockSpec((tm, d), lambda i: (i, 0)),
                  pl.BlockSpec((tm, d_o), lambda i: (i, 0)),
                  resident((d_o, d), mix_layer),
                  resident((1, d)),
                  resident((d, d_ff), layer),
                  resident((d, d_ff), layer),
                  resident((d_ff, d), layer),
                  resident((1, d))],
        out_specs=pl.BlockSpec((tm, d), lambda i: (i, 0)),
        out_shape=jax.ShapeDtypeStruct((m, d), F32),
        scratch_shapes=[pltpu.VMEM((tm, d), BF16), pltpu.VMEM((tm, d), F32)],
        compiler_params=_cparams(("arbitrary",)),
        name="post_attn",
    )(x, o, w_o, g, w_gate, w_up, w_down, g_final)


def _diff_lambda(lam_ref, lam_init):
    lp = lam_ref[...]
    s1 = jnp.sum(lp[0:1] * lp[1:2], axis=-1, keepdims=True)
    s2 = jnp.sum(lp[2:3] * lp[3:4], axis=-1, keepdims=True)
    return jnp.exp(s1) - jnp.exp(s2) + lam_init


def _near_bias(s, bias_ref, mp, n_g, diag):
    parts = []
    for g in range(n_g):
        tile = bias_ref[g * 2 + mp]
        t_prev, t_own = tile[:, :BLK], tile[:, BLK:]
        top = s[g * TQ:g * TQ + BLK]
        bot = s[g * TQ + BLK:(g + 1) * TQ]
        if diag:
            top = jnp.concatenate([top[:, :BLK] + t_own, jnp.full((BLK, BLK), NEG_INF, F32)], axis=1)
            bot = jnp.concatenate([bot[:, :BLK] + t_prev, bot[:, BLK:] + t_own], axis=1)
        else:
            top = jnp.concatenate([top[:, :BLK], top[:, BLK:] + t_prev], axis=1)
        parts += [top, bot]
    return jnp.concatenate(parts, axis=0)


def _diff_prompt_block(n_chunks, q_ref, k_ref, v_ref, bias_ref, lam_ref, subln_ref, o_ref, s_scr, m_scr, v1_scr,
                       *, lam_init, n_g):
    dv = v_ref.shape[1]
    if n_chunks == 1:
        v1_scr[:, :dv] = v_ref[...]
        v1_scr[:, dv:] = jnp.ones((v1_scr.shape[0], v1_scr.shape[1] - dv), v1_scr.dtype)
    lam = _diff_lambda(lam_ref, lam_init)
    o_maps = []
    for mp in range(2):
        q = jnp.concatenate([q_ref[:, (g * 2 + mp) * HEAD_DIM:(g * 2 + mp + 1) * HEAD_DIM]
                             for g in range(n_g)], axis=0)
        ksl = slice(mp * HEAD_DIM, (mp + 1) * HEAD_DIM)

        mx = None
        for c in range(n_chunks):
            s = lax.dot_general(q, k_ref[c * TQ:(c + 1) * TQ, ksl], _NT, preferred_element_type=F32)
            if c >= n_chunks - 2:
                s = _near_bias(s, bias_ref, mp, n_g, diag=(c == n_chunks - 1))
            s_scr[:, c * TQ:(c + 1) * TQ] = s
            cm = jnp.maximum(s[:, :BLK], s[:, BLK:])
            mx = cm if mx is None else jnp.maximum(mx, cm)
        m_scr[...] = jnp.broadcast_to(jnp.max(mx, axis=1, keepdims=True), m_scr.shape)

        p = jnp.concatenate([jnp.exp2(s_scr[:, j * BLK:(j + 1) * BLK] - m_scr[...]).astype(BF16)
                             for j in range(n_chunks * TQ // BLK)], axis=1)
        acc = jnp.dot(p, v1_scr[0:n_chunks * TQ, :], preferred_element_type=F32)
        o_maps.append(acc[:, :dv] / acc[:, dv:dv + 1])

    d = o_maps[0] - lam * o_maps[1]
    d = d * lax.rsqrt(jnp.mean(d * d, axis=-1, keepdims=True) + RMS_EPS) * subln_ref[...]
    d = d * (1.0 - lam_init)
    for g in range(n_g):
        o_ref[:, g * dv:(g + 1) * dv] = d[g * TQ:(g + 1) * TQ].astype(o_ref.dtype)


def _head_of_row(shape, n_heads, n_kv):
    return (lax.broadcasted_iota(jnp.int32, shape, 0) % n_heads) // (n_heads // n_kv)


def _diff_sample_init(m_scr, l_scr, acc_scr):
    m_scr[...] = jnp.full(m_scr.shape, NEG_INF, F32)
    l_scr[...] = jnp.zeros(l_scr.shape, F32)
    acc_scr[...] = jnp.zeros(acc_scr.shape, F32)


def _diff_sample_update(grp, last, q_ref, bias_ref, k_refs, v_refs, m_scr, l_scr, acc_scr, *, n_kv):
    pages = len(k_refs)
    n_rows, dv = q_ref.shape
    n_heads = n_rows // 2
    prows = k_refs[0].shape[0]
    qb = (q_ref[...].astype(F32) * 0.125).astype(BF16)
    own = (_head_of_row((n_rows, prows), n_heads, n_kv)
           == lax.broadcasted_iota(jnp.int32, (n_rows, prows), 1) % n_kv)
    near_bias = jnp.where(grp == last, bias_ref[:, 0:prows], 0.0)
    s_list = []
    for p in range(pages):
        s = lax.dot_general(qb, k_refs[p][...].astype(BF16), _NT, preferred_element_type=F32)
        if p == pages - 1:
            s = s + near_bias
        s_list.append(jnp.where(own, s, NEG_INF))
    s_all = jnp.concatenate(s_list, axis=1)
    m_old = m_scr[...]
    m_new = jnp.maximum(m_old, jnp.max(s_all, axis=1, keepdims=True))
    alpha = jnp.exp(m_old - m_new)
    p_all = jnp.exp(s_all - m_new)
    l_new = alpha * l_scr[...] + jnp.sum(p_all, axis=1, keepdims=True)
    acc = alpha * acc_scr[...]
    for p in range(pages):
        acc = acc + jnp.dot(p_all[:, p * prows:(p + 1) * prows].astype(BF16), v_refs[p][...].astype(BF16),
                            preferred_element_type=F32)
    m_scr[...] = m_new
    l_scr[...] = l_new
    acc_scr[...] = acc


def _diff_sample_finish(q_ref, knew_ref, vnew_ref, bias_ref, lam_ref, subln_ref, o_ref, m_scr, l_scr, acc_scr, *,
                        lam_init, n_kv):
    n_rows, dv = q_ref.shape
    n_heads = n_rows // 2
    self_col = bias_ref.shape[1] - BLK
    qf = q_ref[...].astype(F32) * 0.125
    row_kv = _head_of_row((n_rows, dv), n_heads, n_kv)
    k_self = jnp.zeros((n_rows, dv), F32)
    v_self = jnp.zeros((n_rows, dv), F32)
    for kk in range(n_kv):
        k_self = k_self + jnp.where(row_kv == kk, knew_ref[:, kk * dv:(kk + 1) * dv], 0.0)
        v_self = v_self + jnp.where(row_kv == kk, vnew_ref[:, kk * dv:(kk + 1) * dv], 0.0)
    s_self = jnp.sum(qf * k_self, axis=1, keepdims=True) + bias_ref[:, self_col:self_col + 1]
    m_old = m_scr[...]
    m_f = jnp.maximum(m_old, s_self)
    a = jnp.exp(m_old - m_f)
    p_self = jnp.exp(s_self - m_f)
    l_f = a * l_scr[...] + p_self
    o = (a * acc_scr[...] + p_self * v_self) / l_f
    lam = _diff_lambda(lam_ref, lam_init)
    d = o[:n_heads] - lam * o[n_heads:]
    d = d * lax.rsqrt(jnp.mean(d * d, axis=-1, keepdims=True) + RMS_EPS) * subln_ref[...]
    o_ref[...] = (d * (1.0 - lam_init)).astype(o_ref.dtype)


def _page_copies(pt_ref, ck_ref, cv_ref, kbuf, vbuf, sem, step, slot, li):
    pages = kbuf.shape[1]
    copies = []
    for p in range(pages):
        phys = pt_ref[step * pages + p]
        copies.append(pltpu.make_async_copy(ck_ref.at[phys, li], kbuf.at[slot, p], sem.at[slot, 0]))
        copies.append(pltpu.make_async_copy(cv_ref.at[phys, li], vbuf.at[slot, p], sem.at[slot, 1]))
    return copies


def _diff_attn_kernel(pt_ref, q_ref, k_ref, v_ref, bias_ref, lam_ref, subln_ref,
                      qs_ref, knew_ref, vnew_ref, bias_rows_ref, ck_ref, cv_ref,
                      o_ref, os_ref, s_scr, m_scr, v1_scr, sm_scr, sl_scr, sacc_scr, kbuf, vbuf, sem, *,
                      li, lam_init, n_g, n_kv, n_qblk, n_groups):
    step = pl.program_id(0)
    n_steps = pl.num_programs(0)
    slot = step % 2
    pages = kbuf.shape[1]
    copies = functools.partial(_page_copies, pt_ref, ck_ref, cv_ref, kbuf, vbuf, sem, li=li)

    def start_all(for_step, into_slot):
        for i, c in enumerate(copies(for_step, into_slot)):
            c.start(priority=i % 2)

    pl.when(step == 0)(functools.partial(start_all, step, slot))
    for c in copies(step, slot):
        c.wait()
    pl.when(step + 1 < n_steps)(functools.partial(start_all, step + 1, 1 - slot))

    k_refs = [kbuf.at[slot, p] for p in range(pages)]
    v_refs = [vbuf.at[slot, p] for p in range(pages)]
    grp = step % n_groups
    last = n_groups - 1
    sample_state = (sm_scr, sl_scr, sacc_scr)
    pl.when(grp == 0)(functools.partial(_diff_sample_init, *sample_state))

    def block(n_chunks):
        _diff_sample_update(grp, last, qs_ref, bias_rows_ref, k_refs, v_refs, *sample_state, n_kv=n_kv)
        _diff_prompt_block(n_chunks, q_ref, k_ref, v_ref, bias_ref, lam_ref, subln_ref, o_ref, s_scr, m_scr,
                           v1_scr, lam_init=lam_init, n_g=n_g)

    for blk in range(n_qblk):
        pl.when(step % n_qblk == blk)(functools.partial(block, blk + 1))
    pl.when(grp == last)(functools.partial(
        _diff_sample_finish, qs_ref, knew_ref, vnew_ref, bias_rows_ref, lam_ref, subln_ref, os_ref, *sample_state,
        lam_init=lam_init, n_kv=n_kv))


def _diff_attn(q, kv, bias_causal, q_rows, k_new, v_new, bias_rows, cache_k, cache_v, li, page_table,
               lam_p, subln, lam_init, batch, seq, n_g):
    dec_b, n_rows, dv = q_rows.shape
    n_pool, n_layers, page, n_kv, _ = cache_k.shape
    n_pages = page_table.shape[1]
    n_heads = n_rows // 2
    width = n_kv * dv
    prows = page * n_kv
    qw = n_g * 2 * HEAD_DIM
    n_qblk = seq // TQ
    steps = batch * n_kv * n_qblk
    n_groups = steps // dec_b
    pages = n_pages // n_groups
    assert seq % TQ == 0 and steps % dec_b == 0 and n_pages % n_groups == 0
    q3 = q.reshape(batch, seq, q.shape[1])
    kv3 = kv.reshape(batch, seq, kv.shape[1])
    cache_k = cache_k.reshape(n_pool, n_layers, prows, dv)
    cache_v = cache_v.reshape(n_pool, n_layers, prows, dv)
    pt_flat = page_table.reshape(-1)

    def prompt_map(f):
        return lambda t, pt: f(t // (n_kv * n_qblk), (t // n_qblk) % n_kv, t % n_qblk)

    def seq_map(t, pt):
        return (t // n_groups, 0, 0)

    kern = functools.partial(_diff_attn_kernel, li=li, lam_init=lam_init, n_g=n_g, n_kv=n_kv,
                             n_qblk=n_qblk, n_groups=n_groups)
    grid_spec = pltpu.PrefetchScalarGridSpec(
        num_scalar_prefetch=1,
        grid=(steps,),
        in_specs=[pl.BlockSpec((None, TQ, qw), prompt_map(lambda b, k, i: (b, i, k))),
                  pl.BlockSpec((None, seq, dv), prompt_map(lambda b, k, i: (b, 0, k))),
                  pl.BlockSpec((None, seq, dv), prompt_map(lambda b, k, i: (b, 0, n_kv + k))),
                  pl.BlockSpec((n_g * 2, BLK, 2 * BLK), prompt_map(lambda b, k, i: (k, 0, 0))),
                  pl.BlockSpec((4, HEAD_DIM), lambda t, pt: (0, 0)),
                  pl.BlockSpec((1, dv), lambda t, pt: (0, 0)),
                  pl.BlockSpec((None, n_rows, dv), seq_map),
                  pl.BlockSpec((None, 1, width), seq_map),
                  pl.BlockSpec((None, 1, width), seq_map),
                  pl.BlockSpec(bias_rows.shape, lambda t, pt: (0, 0)),
                  pl.BlockSpec(memory_space=pl.ANY),
                  pl.BlockSpec(memory_space=pl.ANY)],
        out_specs=[pl.BlockSpec((None, TQ, n_g * dv), prompt_map(lambda b, k, i: (b, i, k))),
                   pl.BlockSpec((None, n_heads, dv), seq_map)],
        scratch_shapes=[pltpu.VMEM((n_g * TQ, seq), F32), pltpu.VMEM((n_g * TQ, BLK), F32),
                        pltpu.VMEM((seq, 2 * dv), BF16),
                        pltpu.VMEM((n_rows, 1), F32), pltpu.VMEM((n_rows, 1), F32),
                        pltpu.VMEM((n_rows, dv), F32),
                        pltpu.VMEM((2, pages, prows, dv), F32), pltpu.VMEM((2, pages, prows, dv), F32),
                        pltpu.SemaphoreType.DMA((2, 2))],
    )
    o, o_s = pl.pallas_call(
        kern,
        grid_spec=grid_spec,
        out_shape=[jax.ShapeDtypeStruct((batch, seq, n_kv * n_g * dv), BF16),
                   jax.ShapeDtypeStruct((dec_b, n_heads, dv), BF16)],
        compiler_params=_cparams(("arbitrary",)),
        name="diff_attn",
    )(pt_flat, q3, kv3, kv3, bias_causal, lam_p, subln,
      q_rows, k_new.reshape(dec_b, 1, width), v_new.reshape(dec_b, 1, width), bias_rows, cache_k, cache_v)
    return o.reshape(batch * seq, n_kv * n_g * dv), o_s.reshape(dec_b, n_heads * dv)


def _swa_prompt_kernel(sink_ref, q_ref, kvp_ref, kvc_ref, bias_ref, o_ref, *, n_kv, n_g):
    i = pl.program_id(1)
    col = lax.broadcasted_iota(jnp.int32, (BLK, 2 * BLK), 1)
    has_prev = (col >= BLK) | (i > 0)
    v_off = n_kv * HEAD_DIM
    outs = []
    for kv in range(n_kv):
        ksl = slice(kv * HEAD_DIM, (kv + 1) * HEAD_DIM)
        vsl = slice(v_off + kv * HEAD_DIM, v_off + (kv + 1) * HEAD_DIM)
        kk = jnp.concatenate([kvp_ref[:, ksl], kvc_ref[:, ksl]], axis=0)
        vv = jnp.concatenate([kvp_ref[:, vsl], kvc_ref[:, vsl]], axis=0)
        for g in range(n_g):
            h = kv * n_g + g
            q = q_ref[:, h * HEAD_DIM:(h + 1) * HEAD_DIM] * 0.125
            s = lax.dot_general(q, kk, _NT, preferred_element_type=F32) + bias_ref[h]
            s = jnp.where(has_prev, s, NEG_INF)
            sink = sink_ref[0, h]
            m = jnp.maximum(jnp.max(s, axis=1, keepdims=True), sink)
            p = jnp.exp(s - m)
            denom = jnp.sum(p, axis=1, keepdims=True) + jnp.exp(sink - m)
            o = jnp.dot(p.astype(BF16), vv, preferred_element_type=F32) / denom
            outs.append(o)
    o_ref[...] = jnp.concatenate(outs, axis=1).astype(o_ref.dtype)


def _swa_attn_prompt(q, kv, bias_band, sinks, batch, seq, n_kv, n_g):
    qw = q.shape[1]
    kvw = kv.shape[1]
    q3 = q.reshape(batch, seq, qw)
    kv3 = kv.reshape(batch, seq, kvw)
    kern = functools.partial(_swa_prompt_kernel, n_kv=n_kv, n_g=n_g)
    o = pl.pallas_call(
        kern,
        grid=(batch, seq // BLK),
        in_specs=[pl.BlockSpec(memory_space=pltpu.SMEM),
                  pl.BlockSpec((None, BLK, qw), lambda b, i: (b, i, 0)),
                  pl.BlockSpec((None, BLK, kvw), lambda b, i: (b, jnp.maximum(i - 1, 0), 0)),
                  pl.BlockSpec((None, BLK, kvw), lambda b, i: (b, i, 0)),
                  pl.BlockSpec((n_kv * n_g, BLK, 2 * BLK), lambda b, i: (0, 0, 0))],
        out_specs=pl.BlockSpec((None, BLK, qw), lambda b, i: (b, i, 0)),
        out_shape=jax.ShapeDtypeStruct((batch, seq, qw), BF16),
        compiler_params=_cparams(("arbitrary", "arbitrary")),
        name="swa_attn_prompt",
    )(sinks.reshape(1, -1), q3, kv3, kv3, bias_band)
    return o.reshape(batch * seq, qw)


def _swa_sample_kernel(q_ref, kvnew_ref, kcol_ref, sk_ref, sv_ref, bias_ref, sink_ref, o_ref, nk_ref, nv_ref,
                       *, n_kv, bb):
    n_heads = q_ref.shape[1]
    n_g = n_heads // n_kv
    buf = sk_ref.shape[-1]
    newest = lax.broadcasted_iota(jnp.int32, (HEAD_DIM, buf), 1) == buf - 1
    for s in range(bb):
        for kv in range(n_kv):
            hs = slice(kv * n_g, (kv + 1) * n_g)
            qf = q_ref[s, hs, :].astype(F32) * 0.125
            k_row = kvnew_ref[s, :, kv * HEAD_DIM:(kv + 1) * HEAD_DIM]
            v_row = kvnew_ref[s, :, (n_kv + kv) * HEAD_DIM:(n_kv + kv + 1) * HEAD_DIM]
            st_k = sk_ref[s, kv]
            st_v = sv_ref[s, kv]
            sink = sink_ref[hs, :]
            sc = jnp.dot(qf.astype(BF16), st_k.astype(BF16), preferred_element_type=F32) + bias_ref[hs, 0:BLK]
            s_self = jnp.sum(qf * k_row, axis=1, keepdims=True) + bias_ref[hs, BLK:BLK + 1]
            m = jnp.maximum(jnp.maximum(jnp.max(sc, axis=1, keepdims=True), s_self), sink)
            p = jnp.exp(sc - m)
            p_self = jnp.exp(s_self - m)
            denom = jnp.sum(p, axis=1, keepdims=True) + p_self + jnp.exp(sink - m)
            o = lax.dot_general(p.astype(BF16), st_v.astype(BF16), _NT, preferred_element_type=F32)
            o_ref[s, hs, :] = ((o + p_self * v_row) / denom).astype(o_ref.dtype)
            nk_ref[s, kv] = jnp.where(newest, kcol_ref[s, kv], pltpu.roll(st_k, buf - 1, axis=1))
            nv_ref[s, kv] = jnp.where(newest, kcol_ref[s, n_kv + kv], pltpu.roll(st_v, buf - 1, axis=1))


def _swa_attn_sample(q, kv_new, state_k_t, state_v_t, li, bias_rows, sinks, bb):
    b, n_heads, _ = q.shape
    n_kv, buf = state_k_t.shape[2], state_k_t.shape[4]
    kw = n_kv * HEAD_DIM
    assert buf == WINDOW and b % bb == 0
    kern = functools.partial(_swa_sample_kernel, n_kv=n_kv, bb=bb)
    st_spec = pl.BlockSpec((bb, None, n_kv, HEAD_DIM, buf), lambda i: (i, li, 0, 0, 0))
    new_spec = pl.BlockSpec((bb, n_kv, HEAD_DIM, buf), lambda i: (i, 0, 0, 0))
    new_shape = jax.ShapeDtypeStruct((b, n_kv, HEAD_DIM, buf), F32)
    o, nk, nv = pl.pallas_call(
        kern,
        grid=(b // bb,),
        in_specs=[pl.BlockSpec((bb, n_heads, HEAD_DIM), lambda i: (i, 0, 0)),
                  pl.BlockSpec((bb, 1, 2 * kw), lambda i: (i, 0, 0)),
                  pl.BlockSpec((bb, 2 * n_kv, HEAD_DIM, 1), lambda i: (i, 0, 0, 0)),
                  st_spec, st_spec,
                  pl.BlockSpec((n_heads, 2 * BLK), lambda i: (0, 0)),
                  pl.BlockSpec((n_heads, 1), lambda i: (0, 0))],
        out_specs=[pl.BlockSpec((bb, n_heads, HEAD_DIM), lambda i: (i, 0, 0)), new_spec, new_spec],
        out_shape=[jax.ShapeDtypeStruct((b, n_heads, HEAD_DIM), F32), new_shape, new_shape],
        compiler_params=_cparams(("arbitrary",)),
        name="swa_attn_sample",
    )(q, kv_new.reshape(b, 1, 2 * kw), kv_new.reshape(b, 2 * n_kv, HEAD_DIM, 1), state_k_t, state_v_t,
      bias_rows, sinks.reshape(-1, 1))
    return o.reshape(b, n_heads * HEAD_DIM).astype(BF16), nk, nv


def _block_rows(q_rows, n_kv_blocks, block_of_row):
    w = q_rows.shape[-1]
    tiled = jnp.tile(q_rows, (1, 1, n_kv_blocks))
    lane_block = np.arange(n_kv_blocks * w)[None, :] // w
    mask = jnp.asarray(lane_block == np.asarray(block_of_row)[:, None])
    return jnp.where(mask[None], tiled, jnp.zeros_like(tiled))


def kernel(x_prompt, x_sample, cache_k_diff, cache_v_diff, state_k_swa, state_v_swa, page_table,
           rel_bias, g_mix, g_ffn, g_final, w_qkv_diff, w_o_diff, lambda_diff, subln_diff,
           w_qkv_swa, b_qkv_swa, w_o_swa, sinks_swa, w_gate, w_up, w_down):
    batch, seq, d_model = x_prompt.shape
    dec_b = x_sample.shape[0]
    depth = g_mix.shape[0]
    n_pool, n_diff_layers, page, n_kv_diff, kd = cache_k_diff.shape
    n_heads_diff = w_o_diff.shape[1] // kd
    g_diff = n_heads_diff // n_kv_diff
    n_kv_swa = state_k_swa.shape[3]
    n_heads_swa = sinks_swa.shape[1]
    g_swa = n_heads_swa // n_kv_swa
    buf = state_k_swa.shape[2]
    q_diff = n_heads_diff * 2 * HEAD_DIM
    q_swa = n_heads_swa * HEAD_DIM
    kw_diff = n_kv_diff * kd
    kw_swa = n_kv_swa * HEAD_DIM

    bias_causal, bias_causal2, bias_band = _bias_tiles(rel_bias)
    diff_cols = np.array([2 * (r % n_heads_diff) + r // n_heads_diff for r in range(2 * n_heads_diff)])
    diff_row_map = np.arange(2 * n_heads_diff) // n_heads_diff
    near = bias_causal[:, 0, :][diff_cols]
    bias_rows_diff = jnp.concatenate([jnp.repeat(near[:, :BLK], n_kv_diff, axis=1), near[:, BLK:]], axis=1)
    bias_rows_swa = bias_band[:, 0, :]

    st_k = state_k_swa.transpose(0, 1, 3, 4, 2)
    st_v = state_v_swa.transpose(0, 1, 3, 4, 2)

    xp = x_prompt.reshape(batch * seq, d_model)
    xs = x_sample.reshape(dec_b, d_model)
    tm_p, tm_s = 512, dec_b
    zero_bias = jnp.zeros((1, w_qkv_diff.shape[2]), F32)
    wq_diff, wo_diff = w_qkv_diff.astype(BF16), w_o_diff.astype(BF16)
    wq_swa, wo_swa = w_qkv_swa.astype(BF16), w_o_swa.astype(BF16)
    wg, wu, wd = w_gate.astype(BF16), w_up.astype(BF16), w_down.astype(BF16)
    kds, vds, ksp, vsp, kss, vss = [], [], [], [], [], []
    new_cache = None

    for i in range(depth):
        li = i // 2
        g_m = g_mix[i].reshape(1, d_model)
        if i % 2 == 0:
            lam_init = 0.8 - 0.6 * math.exp(-0.3 * i)
            w_o = wo_diff
            subln = subln_diff[li].reshape(1, kd)
            qp, kvp_b, k_cache, v_cache = _qkv_proj_paged(xp, g_m, wq_diff, q_diff, n_kv_diff, tm_p, batch, li,
                                                          new_cache)
            new_cache = (k_cache, v_cache)
            qs, kvs_f, _ = _qkv_proj(xs, g_m, wq_diff, li, zero_bias, q_diff, tm_s)
            q_rows = qs.reshape(dec_b, n_kv_diff, g_diff, 2, HEAD_DIM).transpose(0, 3, 1, 2, 4)
            q_rows = _block_rows(q_rows.reshape(dec_b, 2 * n_heads_diff, HEAD_DIM), 2, diff_row_map)
            op, os_ = _diff_attn(qp, kvp_b, bias_causal2, q_rows, kvs_f[:, :kw_diff], kvs_f[:, kw_diff:],
                                 bias_rows_diff, cache_k_diff, cache_v_diff, li, page_table,
                                 lambda_diff[li], subln, lam_init, batch, seq, g_diff)
            kds.append(kvs_f[:, :kw_diff].reshape(dec_b, 1, n_kv_diff, kd))
            vds.append(kvs_f[:, kw_diff:].reshape(dec_b, 1, n_kv_diff, kd))
        else:
            w_o = wo_swa
            b_qkv = b_qkv_swa[li].reshape(1, -1)
            qp, kvp_f, kvp_b = _qkv_proj(xp, g_m, wq_swa, li, b_qkv, q_swa, tm_p)
            qs, kvs_f, _ = _qkv_proj(xs, g_m, wq_swa, li, b_qkv, q_swa, tm_s)
            op = _swa_attn_prompt(qp, kvp_b, bias_band, sinks_swa[li], batch, seq, n_kv_swa, g_swa)
            os_, nk, nv = _swa_attn_sample(qs.reshape(dec_b, n_heads_swa, HEAD_DIM).astype(F32), kvs_f,
                                           st_k, st_v, li, bias_rows_swa, sinks_swa[li], bb=8)
            tail = kvp_f.reshape(batch, seq, 2 * kw_swa)[:, seq - min(WINDOW, seq):]
            ksp.append(tail[..., :kw_swa].reshape(batch, -1, n_kv_swa, HEAD_DIM))
            vsp.append(tail[..., kw_swa:].reshape(batch, -1, n_kv_swa, HEAD_DIM))
            kss.append(nk.transpose(0, 3, 1, 2))
            vss.append(nv.transpose(0, 3, 1, 2))
        final = i == depth - 1
        g_f = g_ffn[i].reshape(1, d_model)
        gfin = g_final.reshape(1, d_model)
        xp = _post_attn(xp, op, w_o, li, g_f, wg, wu, wd, i, gfin, tm_p, final)
        xs = _post_attn(xs, os_, w_o, li, g_f, wg, wu, wd, i, gfin, tm_s, final)

    k_cache, v_cache = new_cache
    cache_shape = (batch, n_diff_layers, seq, n_kv_diff, kd)
    return (xp.reshape(batch, seq, d_model), xs.reshape(dec_b, 1, d_model),
            k_cache.reshape(cache_shape), v_cache.reshape(cache_shape),
            jnp.stack(kds, axis=1), jnp.stack(vds, axis=1),
            jnp.stack(ksp, axis=1), jnp.stack(vsp, axis=1), jnp.stack(kss, axis=1), jnp.stack(vss, axis=1))
```

```python
import functools
import math

import numpy as np
import jax
import jax.numpy as jnp
from jax import lax
from jax.experimental import pallas as pl
from jax.experimental.pallas import tpu as pltpu

F32 = jnp.float32
BF16 = jnp.bfloat16

HEAD_DIM = 64
WINDOW = 128
NUM_BUCKETS = 32
MAX_DISTANCE = 128
RMS_EPS = 1e-6
NEG_INF = -1e30
LOG2E = math.log2(math.e)
BLK = 128
TQ = 2 * BLK
LANES = 128
VMEM_LIMIT = 56 * 1024 * 1024

_NT = (((1,), (1,)), ((), ()))


def _rms(x, g):
    return x * lax.rsqrt(jnp.mean(x * x, axis=-1, keepdims=True) + RMS_EPS) * g


def _cparams(sem):
    return pltpu.CompilerParams(dimension_semantics=sem, vmem_limit_bytes=VMEM_LIMIT)


def _bucket_table():
    dist = np.arange(BLK)[:, None] + BLK - np.arange(2 * BLK)[None, :]
    n = np.maximum(dist, 0)
    max_exact = NUM_BUCKETS // 2
    nf = np.maximum(n, 1).astype(np.float32)
    large = max_exact + (np.log(nf / np.float32(max_exact)) / np.float32(math.log(MAX_DISTANCE / max_exact))
                         * np.float32(NUM_BUCKETS - max_exact)).astype(np.int32)
    large = np.minimum(large, NUM_BUCKETS - 1)
    return np.where(n < max_exact, n, large).astype(np.int32)


def _bias_kernel(table_ref, bucket_ref, causal_ref, causal2_ref, band_ref):
    col = pl.program_id(0)
    bucket = bucket_ref[...]
    t = jnp.zeros(bucket.shape, F32)
    for b in range(NUM_BUCKETS):
        t = jnp.where(bucket == b, table_ref[b, col], t)
    far = table_ref[NUM_BUCKETS - 1, col]
    dist = (lax.broadcasted_iota(jnp.int32, bucket.shape, 0) + BLK
            - lax.broadcasted_iota(jnp.int32, bucket.shape, 1))
    causal_ref[...] = jnp.where(dist >= 0, t - far, NEG_INF)
    causal2_ref[...] = jnp.where(dist >= 0, (t - far) * LOG2E, NEG_INF)
    band_ref[...] = jnp.where((dist >= 0) & (dist < WINDOW), t, NEG_INF)


def _bias_tiles(rel_bias):
    n_cols = rel_bias.shape[1]
    bucket = jnp.asarray(_bucket_table())
    shape = jax.ShapeDtypeStruct((n_cols, BLK, 2 * BLK), F32)
    return pl.pallas_call(
        _bias_kernel,
        grid=(n_cols,),
        in_specs=[pl.BlockSpec(memory_space=pltpu.SMEM),
                  pl.BlockSpec((BLK, 2 * BLK), lambda c: (0, 0))],
        out_specs=[pl.BlockSpec((None, BLK, 2 * BLK), lambda c: (c, 0, 0))] * 3,
        out_shape=[shape] * 3,
        compiler_params=_cparams(("arbitrary",)),
        name="bias_tiles",
    )(rel_bias, bucket)


def _qkv_kernel(x_ref, g_ref, w_ref, b_ref, q_ref, kvf_ref, kvb_ref, h_scr, *, n_q, chunk):
    h_scr[...] = _rms(x_ref[...], g_ref[...]).astype(BF16)
    n_total = w_ref.shape[1]
    for c0 in range(0, n_total, chunk):
        c1 = min(c0 + chunk, n_total)
        acc = jnp.dot(h_scr[...], w_ref[:, c0:c1], preferred_element_type=F32) + b_ref[:, c0:c1]
        if c1 <= n_q:
            q_ref[:, c0:c1] = acc.astype(BF16)
        else:
            kvf_ref[:, c0 - n_q:c1 - n_q] = acc
            kvb_ref[:, c0 - n_q:c1 - n_q] = acc.astype(BF16)


def _qkv_proj(x, g, w, layer, b, n_q, tm):
    m, d = x.shape
    n = w.shape[2]
    n_kv = n - n_q
    chunk = 256
    assert m % tm == 0 and n_q % chunk == 0 and n_kv % chunk == 0
    kern = functools.partial(_qkv_kernel, n_q=n_q, chunk=chunk)
    return pl.pallas_call(
        kern,
        grid=(m // tm,),
        in_specs=[pl.BlockSpec((tm, d), lambda i: (i, 0)),
                  pl.BlockSpec((1, d), lambda i: (0, 0)),
                  pl.BlockSpec((None, d, n), lambda i: (layer, 0, 0)),
                  pl.BlockSpec((1, n), lambda i: (0, 0))],
        out_specs=[pl.BlockSpec((tm, n_q), lambda i: (i, 0)),
                   pl.BlockSpec((tm, n_kv), lambda i: (i, 0)),
                   pl.BlockSpec((tm, n_kv), lambda i: (i, 0))],
        out_shape=[jax.ShapeDtypeStruct((m, n_q), BF16),
                   jax.ShapeDtypeStruct((m, n_kv), F32),
                   jax.ShapeDtypeStruct((m, n_kv), BF16)],
        scratch_shapes=[pltpu.VMEM((tm, d), BF16)],
        compiler_params=_cparams(("arbitrary",)),
        name="qkv_proj",
    )(x, g, w, b)


def _qkv_paged_kernel(x_ref, g_ref, w_ref, *rest, n_q, n_kv, chunk, first):
    q_ref, kvb_ref, ko_ref, vo_ref, h_scr = rest if first else rest[2:]
    tm = x_ref.shape[0]
    dv = ko_ref.shape[-1]
    h_scr[...] = _rms(x_ref[...], g_ref[...]).astype(BF16)
    n_total = w_ref.shape[1]
    for c0 in range(0, n_total, chunk):
        acc = jnp.dot(h_scr[...], w_ref[:, c0:c0 + chunk], preferred_element_type=F32)
        if c0 < n_q:
            q_ref[:, c0:c0 + chunk] = (acc * (HEAD_DIM ** -0.5 * LOG2E)).astype(BF16)
            continue
        kvb_ref[:, c0 - n_q:c0 - n_q + chunk] = acc.astype(BF16)
        for j in range(chunk // dv):
            head = (c0 - n_q) // dv + j
            dst = ko_ref if head < n_kv else vo_ref
            rows = pl.ds(head % n_kv, tm, stride=n_kv)
            val = acc[:, j * dv:(j + 1) * dv]
            if first:
                for layer in range(dst.shape[0]):
                    dst[layer, rows, :] = val
            else:
                dst[rows, :] = val


def _qkv_proj_paged(x, g, w, n_q, n_kv, tm, batch, layer, prev):
    m, d = x.shape
    n_layers, _, n = w.shape
    seq = m // batch
    dv = (n - n_q) // (2 * n_kv)
    chunk = 256
    first = prev is None
    assert seq % tm == 0 and n_q % chunk == 0 and (n - n_q) % chunk == 0 and chunk % dv == 0
    tiles = seq // tm
    kern = functools.partial(_qkv_paged_kernel, n_q=n_q, n_kv=n_kv, chunk=chunk, first=first)
    cache_shape = jax.ShapeDtypeStruct((batch, n_layers, seq * n_kv, dv), F32)
    if first:
        cache_spec = pl.BlockSpec((None, n_layers, tm * n_kv, dv), lambda i: (i // tiles, 0, i % tiles, 0))
        extra_in, extra_args, aliases = [], [], {}
    else:
        cache_spec = pl.BlockSpec((None, None, tm * n_kv, dv), lambda i: (i // tiles, layer, i % tiles, 0))
        extra_in = [pl.BlockSpec(memory_space=pl.ANY)] * 2
        extra_args, aliases = list(prev), {3: 2, 4: 3}
    return pl.pallas_call(
        kern,
        grid=(m // tm,),
        in_specs=[pl.BlockSpec((tm, d), lambda i: (i, 0)),
                  pl.BlockSpec((1, d), lambda i: (0, 0)),
                  pl.BlockSpec((None, d, n), lambda i: (layer, 0, 0))] + extra_in,
        out_specs=[pl.BlockSpec((tm, n_q), lambda i: (i, 0)),
                   pl.BlockSpec((tm, n - n_q), lambda i: (i, 0)),
                   cache_spec, cache_spec],
        out_shape=[jax.ShapeDtypeStruct((m, n_q), BF16),
                   jax.ShapeDtypeStruct((m, n - n_q), BF16),
                   cache_shape, cache_shape],
        scratch_shapes=[pltpu.VMEM((tm, d), BF16)],
        input_output_aliases=aliases,
        compiler_params=_cparams(("arbitrary",)),
        name="qkv_proj_paged",
    )(x, g, w, *extra_args)


def _post_kernel(x_ref, o_ref, wo_ref, g_ref, wg_ref, wu_ref, wd_ref, gf_ref, out_ref,
                 h_scr, acc_scr, *, chunk, final):
    x1 = x_ref[...] + jnp.dot(o_ref[...], wo_ref[...], preferred_element_type=F32)
    acc_scr[...] = x1
    h_scr[...] = _rms(x1, g_ref[...]).astype(BF16)
    d_ff = wg_ref.shape[1]
    for c0 in range(0, d_ff, chunk):
        gate = jnp.dot(h_scr[...], wg_ref[:, c0:c0 + chunk], preferred_element_type=F32)
        up = jnp.dot(h_scr[...], wu_ref[:, c0:c0 + chunk], preferred_element_type=F32)
        act = (gate * (1.0 / (1.0 + jnp.exp(-gate))) * up).astype(BF16)
        acc_scr[...] += jnp.dot(act, wd_ref[c0:c0 + chunk, :], preferred_element_type=F32)
    if final:
        out_ref[...] = _rms(acc_scr[...], gf_ref[...])
    else:
        out_ref[...] = acc_scr[...]


def _post_attn(x, o, w_o, mix_layer, g, w_gate, w_up, w_down, layer, g_final, tm, final):
    m, d = x.shape
    d_o = o.shape[1]
    d_ff = w_gate.shape[2]
    chunk = 256
    assert m % tm == 0 and d_ff % chunk == 0
    kern = functools.partial(_post_kernel, chunk=chunk, final=final)

    def resident(shape, slot=None):
        if slot is None:
            return pl.BlockSpec(shape, lambda i: (0, 0), pipeline_mode=pl.Buffered(1))
        return pl.BlockSpec((None,) + shape, lambda i: (slot, 0, 0), pipeline_mode=pl.Buffered(1))

    return pl.pallas_call(
        kern,
        grid=(m // tm,),
        in_specs=[pl.BlockSpec((tm, d), lambda i: (i, 0)),
                  pl.BlockSpec((tm, d_o), lambda i: (i, 0)),
                  resident((d_o, d), mix_layer),
                  resident((1, d)),
                  resident((d, d_ff), layer),
                  resident((d, d_ff), layer),
                  resident((d_ff, d), layer),
                  resident((1, d))],
        out_specs=pl.BlockSpec((tm, d), lambda i: (i, 0)),
        out_shape=jax.ShapeDtypeStruct((m, d), F32),
        scratch_shapes=[pltpu.VMEM((tm, d), BF16), pltpu.VMEM((tm, d), F32)],
        compiler_params=_cparams(("arbitrary",)),
        name="post_attn",
    )(x, o, w_o, g, w_gate, w_up, w_down, g_final)


def _diff_lambda(lam_ref, lam_init):
    lp = lam_ref[...]
    s1 = jnp.sum(lp[0:1] * lp[1:2], axis=-1, keepdims=True)
    s2 = jnp.sum(lp[2:3] * lp[3:4], axis=-1, keepdims=True)
    return jnp.exp(s1) - jnp.exp(s2) + lam_init


def _near_bias(s, bias_ref, mp, n_g, diag):
    parts = []
    for g in range(n_g):
        tile = bias_ref[g * 2 + mp]
        t_prev, t_own = tile[:, :BLK], tile[:, BLK:]
        top = s[g * TQ:g * TQ + BLK]
        bot = s[g * TQ + BLK:(g + 1) * TQ]
        if diag:
            top = jnp.concatenate([top[:, :BLK] + t_own, jnp.full((BLK, BLK), NEG_INF, F32)], axis=1)
            bot = jnp.concatenate([bot[:, :BLK] + t_prev, bot[:, BLK:] + t_own], axis=1)
        else:
            top = jnp.concatenate([top[:, :BLK], top[:, BLK:] + t_prev], axis=1)
        parts += [top, bot]
    return jnp.concatenate(parts, axis=0)


def _diff_prompt_block(n_chunks, q_ref, k_ref, v_ref, bias_ref, lam_ref, subln_ref, o_ref, s_scr, m_scr, v1_scr,
                       *, lam_init, n_g):
    dv = v_ref.shape[1]
    if n_chunks == 1:
        v1_scr[:, :dv] = v_ref[...]
        v1_scr[:, dv:] = jnp.ones((v1_scr.shape[0], v1_scr.shape[1] - dv), v1_scr.dtype)
    lam = _diff_lambda(lam_ref, lam_init)
    o_maps = []
    for mp in range(2):
        q = jnp.concatenate([q_ref[:, (g * 2 + mp) * HEAD_DIM:(g * 2 + mp + 1) * HEAD_DIM]
                             for g in range(n_g)], axis=0)
        ksl = slice(mp * HEAD_DIM, (mp + 1) * HEAD_DIM)

        mx = None
        for c in range(n_chunks):
            s = lax.dot_general(q, k_ref[c * TQ:(c + 1) * TQ, ksl], _NT, preferred_element_type=F32)
            if c >= n_chunks - 2:
                s = _near_bias(s, bias_ref, mp, n_g, diag=(c == n_chunks - 1))
            s_scr[:, c * TQ:(c + 1) * TQ] = s
            cm = jnp.maximum(s[:, :BLK], s[:, BLK:])
            mx = cm if mx is None else jnp.maximum(mx, cm)
        m_scr[...] = jnp.broadcast_to(jnp.max(mx, axis=1, keepdims=True), m_scr.shape)

        acc = None
        for c in range(n_chunks):
            p = jnp.concatenate([jnp.exp2(s_scr[:, c * TQ + j * BLK:c * TQ + (j + 1) * BLK] - m_scr[...]).astype(BF16)
                                 for j in range(TQ // BLK)], axis=1)
            pv = jnp.dot(p, v1_scr[c * TQ:(c + 1) * TQ, :], preferred_element_type=F32)
            acc = pv if acc is None else acc + pv
        o_maps.append(acc[:, :dv] / acc[:, dv:dv + 1])

    d = o_maps[0] - lam * o_maps[1]
    d = d * lax.rsqrt(jnp.mean(d * d, axis=-1, keepdims=True) + RMS_EPS) * subln_ref[...]
    d = d * (1.0 - lam_init)
    for g in range(n_g):
        o_ref[:, g * dv:(g + 1) * dv] = d[g * TQ:(g + 1) * TQ].astype(o_ref.dtype)


def _head_of_row(shape, n_heads, n_kv):
    return (lax.broadcasted_iota(jnp.int32, shape, 0) % n_heads) // (n_heads // n_kv)


def _diff_sample_init(m_scr, l_scr, acc_scr):
    m_scr[...] = jnp.full(m_scr.shape, NEG_INF, F32)
    l_scr[...] = jnp.zeros(l_scr.shape, F32)
    acc_scr[...] = jnp.zeros(acc_scr.shape, F32)


def _diff_sample_update(grp, last, q_ref, bias_ref, k_refs, v_refs, m_scr, l_scr, acc_scr, *, n_kv):
    pages = len(k_refs)
    n_rows, dv = q_ref.shape
    n_heads = n_rows // 2
    prows = k_refs[0].shape[0]
    qb = (q_ref[...].astype(F32) * 0.125).astype(BF16)
    own = (_head_of_row((n_rows, prows), n_heads, n_kv)
           == lax.broadcasted_iota(jnp.int32, (n_rows, prows), 1) % n_kv)
    near_bias = jnp.where(grp == last, bias_ref[:, 0:prows], 0.0)
    s_list = []
    for p in range(pages):
        s = lax.dot_general(qb, k_refs[p][...].astype(BF16), _NT, preferred_element_type=F32)
        if p == pages - 1:
            s = s + near_bias
        s_list.append(jnp.where(own, s, NEG_INF))
    s_all = jnp.concatenate(s_list, axis=1)
    m_old = m_scr[...]
    m_new = jnp.maximum(m_old, jnp.max(s_all, axis=1, keepdims=True))
    alpha = jnp.exp(m_old - m_new)
    p_all = jnp.exp(s_all - m_new)
    l_new = alpha * l_scr[...] + jnp.sum(p_all, axis=1, keepdims=True)
    acc = alpha * acc_scr[...]
    for p in range(pages):
        acc = acc + jnp.dot(p_all[:, p * prows:(p + 1) * prows].astype(BF16), v_refs[p][...].astype(BF16),
                            preferred_element_type=F32)
    m_scr[...] = m_new
    l_scr[...] = l_new
    acc_scr[...] = acc


def _diff_sample_finish(q_ref, knew_ref, vnew_ref, bias_ref, lam_ref, subln_ref, o_ref, m_scr, l_scr, acc_scr, *,
                        lam_init, n_kv):
    n_rows, dv = q_ref.shape
    n_heads = n_rows // 2
    self_col = bias_ref.shape[1] - BLK
    qf = q_ref[...].astype(F32) * 0.125
    row_kv = _head_of_row((n_rows, dv), n_heads, n_kv)
    k_self = jnp.zeros((n_rows, dv), F32)
    v_self = jnp.zeros((n_rows, dv), F32)
    for kk in range(n_kv):
        k_self = k_self + jnp.where(row_kv == kk, knew_ref[:, kk * dv:(kk + 1) * dv], 0.0)
        v_self = v_self + jnp.where(row_kv == kk, vnew_ref[:, kk * dv:(kk + 1) * dv], 0.0)
    s_self = jnp.sum(qf * k_self, axis=1, keepdims=True) + bias_ref[:, self_col:self_col + 1]
    m_old = m_scr[...]
    m_f = jnp.maximum(m_old, s_self)
    a = jnp.exp(m_old - m_f)
    p_self = jnp.exp(s_self - m_f)
    l_f = a * l_scr[...] + p_self
    o = (a * acc_scr[...] + p_self * v_self) / l_f
    lam = _diff_lambda(lam_ref, lam_init)
    d = o[:n_heads] - lam * o[n_heads:]
    d = d * lax.rsqrt(jnp.mean(d * d, axis=-1, keepdims=True) + RMS_EPS) * subln_ref[...]
    o_ref[...] = (d * (1.0 - lam_init)).astype(o_ref.dtype)


def _page_copies(pt_ref, ck_ref, cv_ref, kbuf, vbuf, sem, step, slot, li):
    pages = kbuf.shape[1]
    copies = []
    for p in range(pages):
        phys = pt_ref[step * pages + p]
        copies.append(pltpu.make_async_copy(ck_ref.at[phys, li], kbuf.at[slot, p], sem.at[slot, 0]))
        copies.append(pltpu.make_async_copy(cv_ref.at[phys, li], vbuf.at[slot, p], sem.at[slot, 1]))
    return copies


def _diff_attn_kernel(pt_ref, q_ref, k_ref, v_ref, bias_ref, lam_ref, subln_ref,
                      qs_ref, knew_ref, vnew_ref, bias_rows_ref, ck_ref, cv_ref,
                      o_ref, os_ref, s_scr, m_scr, v1_scr, sm_scr, sl_scr, sacc_scr, kbuf, vbuf, sem, *,
                      li, lam_init, n_g, n_kv, n_qblk, n_groups):
    step = pl.program_id(0)
    n_steps = pl.num_programs(0)
    slot = step % 2
    pages = kbuf.shape[1]
    copies = functools.partial(_page_copies, pt_ref, ck_ref, cv_ref, kbuf, vbuf, sem, li=li)

    @pl.when(step == 0)
    def _():
        for c in copies(step, slot):
            c.start()

    for c in copies(step, slot):
        c.wait()

    @pl.when(step + 1 < n_steps)
    def _():
        for c in copies(step + 1, 1 - slot):
            c.start()

    k_refs = [kbuf.at[slot, p] for p in range(pages)]
    v_refs = [vbuf.at[slot, p] for p in range(pages)]
    grp = step % n_groups
    last = n_groups - 1
    sample_state = (sm_scr, sl_scr, sacc_scr)
    pl.when(grp == 0)(functools.partial(_diff_sample_init, *sample_state))

    def block(n_chunks):
        _diff_sample_update(grp, last, qs_ref, bias_rows_ref, k_refs, v_refs, *sample_state, n_kv=n_kv)
        _diff_prompt_block(n_chunks, q_ref, k_ref, v_ref, bias_ref, lam_ref, subln_ref, o_ref, s_scr, m_scr,
                           v1_scr, lam_init=lam_init, n_g=n_g)

    for blk in range(n_qblk):
        pl.when(step % n_qblk == blk)(functools.partial(block, blk + 1))
    pl.when(grp == last)(functools.partial(
        _diff_sample_finish, qs_ref, knew_ref, vnew_ref, bias_rows_ref, lam_ref, subln_ref, os_ref, *sample_state,
        lam_init=lam_init, n_kv=n_kv))


def _diff_attn(q, kv, bias_causal, q_rows, k_new, v_new, bias_rows, cache_k, cache_v, li, page_table,
               lam_p, subln, lam_init, batch, seq, n_g):
    dec_b, n_rows, dv = q_rows.shape
    n_pool, n_layers, page, n_kv, _ = cache_k.shape
    n_pages = page_table.shape[1]
    n_heads = n_rows // 2
    width = n_kv * dv
    prows = page * n_kv
    qw = n_g * 2 * HEAD_DIM
    n_qblk = seq // TQ
    steps = batch * n_kv * n_qblk
    n_groups = steps // dec_b
    pages = n_pages // n_groups
    assert seq % TQ == 0 and steps % dec_b == 0 and n_pages % n_groups == 0
    q3 = q.reshape(batch, seq, q.shape[1])
    kv3 = kv.reshape(batch, seq, kv.shape[1])
    cache_k = cache_k.reshape(n_pool, n_layers, prows, dv)
    cache_v = cache_v.reshape(n_pool, n_layers, prows, dv)
    pt_flat = page_table.reshape(-1)

    def prompt_map(f):
        return lambda t, pt: f(t // (n_kv * n_qblk), (t // n_qblk) % n_kv, t % n_qblk)

    def seq_map(t, pt):
        return (t // n_groups, 0, 0)

    kern = functools.partial(_diff_attn_kernel, li=li, lam_init=lam_init, n_g=n_g, n_kv=n_kv,
                             n_qblk=n_qblk, n_groups=n_groups)
    grid_spec = pltpu.PrefetchScalarGridSpec(
        num_scalar_prefetch=1,
        grid=(steps,),
        in_specs=[pl.BlockSpec((None, TQ, qw), prompt_map(lambda b, k, i: (b, i, k))),
                  pl.BlockSpec((None, seq, dv), prompt_map(lambda b, k, i: (b, 0, k))),
                  pl.BlockSpec((None, seq, dv), prompt_map(lambda b, k, i: (b, 0, n_kv + k))),
                  pl.BlockSpec((n_g * 2, BLK, 2 * BLK), prompt_map(lambda b, k, i: (k, 0, 0))),
                  pl.BlockSpec((4, HEAD_DIM), lambda t, pt: (0, 0)),
                  pl.BlockSpec((1, dv), lambda t, pt: (0, 0)),
                  pl.BlockSpec((None, n_rows, dv), seq_map),
                  pl.BlockSpec((None, 1, width), seq_map),
                  pl.BlockSpec((None, 1, width), seq_map),
                  pl.BlockSpec(bias_rows.shape, lambda t, pt: (0, 0)),
                  pl.BlockSpec(memory_space=pl.ANY),
                  pl.BlockSpec(memory_space=pl.ANY)],
        out_specs=[pl.BlockSpec((None, TQ, n_g * dv), prompt_map(lambda b, k, i: (b, i, k))),
                   pl.BlockSpec((None, n_heads, dv), seq_map)],
        scratch_shapes=[pltpu.VMEM((n_g * TQ, seq), F32), pltpu.VMEM((n_g * TQ, BLK), F32),
                        pltpu.VMEM((seq, 2 * dv), BF16),
                        pltpu.VMEM((n_rows, 1), F32), pltpu.VMEM((n_rows, 1), F32),
                        pltpu.VMEM((n_rows, dv), F32),
                        pltpu.VMEM((2, pages, prows, dv), F32), pltpu.VMEM((2, pages, prows, dv), F32),
                        pltpu.SemaphoreType.DMA((2, 2))],
    )
    o, o_s = pl.pallas_call(
        kern,
        grid_spec=grid_spec,
        out_shape=[jax.ShapeDtypeStruct((batch, seq, n_kv * n_g * dv), BF16),
                   jax.ShapeDtypeStruct((dec_b, n_heads, dv), BF16)],
        compiler_params=_cparams(("arbitrary",)),
        name="diff_attn",
    )(pt_flat, q3, kv3, kv3, bias_causal, lam_p, subln,
      q_rows, k_new.reshape(dec_b, 1, width), v_new.reshape(dec_b, 1, width), bias_rows, cache_k, cache_v)
    return o.reshape(batch * seq, n_kv * n_g * dv), o_s.reshape(dec_b, n_heads * dv)


def _swa_prompt_kernel(sink_ref, q_ref, kvp_ref, kvc_ref, bias_ref, o_ref, *, n_kv, n_g):
    i = pl.program_id(1)
    col = lax.broadcasted_iota(jnp.int32, (BLK, 2 * BLK), 1)
    has_prev = (col >= BLK) | (i > 0)
    v_off = n_kv * HEAD_DIM
    outs = []
    for kv in range(n_kv):
        ksl = slice(kv * HEAD_DIM, (kv + 1) * HEAD_DIM)
        vsl = slice(v_off + kv * HEAD_DIM, v_off + (kv + 1) * HEAD_DIM)
        kk = jnp.concatenate([kvp_ref[:, ksl], kvc_ref[:, ksl]], axis=0)
        vv = jnp.concatenate([kvp_ref[:, vsl], kvc_ref[:, vsl]], axis=0)
        for g in range(n_g):
            h = kv * n_g + g
            q = q_ref[:, h * HEAD_DIM:(h + 1) * HEAD_DIM] * 0.125
            s = lax.dot_general(q, kk, _NT, preferred_element_type=F32) + bias_ref[h]
            s = jnp.where(has_prev, s, NEG_INF)
            sink = sink_ref[0, h]
            m = jnp.maximum(jnp.max(s, axis=1, keepdims=True), sink)
            p = jnp.exp(s - m)
            denom = jnp.sum(p, axis=1, keepdims=True) + jnp.exp(sink - m)
            o = jnp.dot(p.astype(BF16), vv, preferred_element_type=F32) / denom
            outs.append(o)
    o_ref[...] = jnp.concatenate(outs, axis=1).astype(o_ref.dtype)


def _swa_attn_prompt(q, kv, bias_band, sinks, batch, seq, n_kv, n_g):
    qw = q.shape[1]
    kvw = kv.shape[1]
    q3 = q.reshape(batch, seq, qw)
    kv3 = kv.reshape(batch, seq, kvw)
    kern = functools.partial(_swa_prompt_kernel, n_kv=n_kv, n_g=n_g)
    o = pl.pallas_call(
        kern,
        grid=(batch, seq // BLK),
        in_specs=[pl.BlockSpec(memory_space=pltpu.SMEM),
                  pl.BlockSpec((None, BLK, qw), lambda b, i: (b, i, 0)),
                  pl.BlockSpec((None, BLK, kvw), lambda b, i: (b, jnp.maximum(i - 1, 0), 0)),
                  pl.BlockSpec((None, BLK, kvw), lambda b, i: (b, i, 0)),
                  pl.BlockSpec((n_kv * n_g, BLK, 2 * BLK), lambda b, i: (0, 0, 0))],
        out_specs=pl.BlockSpec((None, BLK, qw), lambda b, i: (b, i, 0)),
        out_shape=jax.ShapeDtypeStruct((batch, seq, qw), BF16),
        compiler_params=_cparams(("arbitrary", "arbitrary")),
        name="swa_attn_prompt",
    )(sinks.reshape(1, -1), q3, kv3, kv3, bias_band)
    return o.reshape(batch * seq, qw)


def _swa_sample_kernel(q_ref, kvnew_ref, kcol_ref, sk_ref, sv_ref, bias_ref, sink_ref, o_ref, nk_ref, nv_ref,
                       *, n_kv, bb):
    n_heads = q_ref.shape[1]
    n_g = n_heads // n_kv
    buf = sk_ref.shape[-1]
    newest = lax.broadcasted_iota(jnp.int32, (HEAD_DIM, buf), 1) == buf - 1
    for s in range(bb):
        for kv in range(n_kv):
            hs = slice(kv * n_g, (kv + 1) * n_g)
            qf = q_ref[s, hs, :].astype(F32) * 0.125
            k_row = kvnew_ref[s, :, kv * HEAD_DIM:(kv + 1) * HEAD_DIM]
            v_row = kvnew_ref[s, :, (n_kv + kv) * HEAD_DIM:(n_kv + kv + 1) * HEAD_DIM]
            st_k = sk_ref[s, kv]
            st_v = sv_ref[s, kv]
            sink = sink_ref[hs, :]
            sc = jnp.dot(qf.astype(BF16), st_k.astype(BF16), preferred_element_type=F32) + bias_ref[hs, 0:BLK]
            s_self = jnp.sum(qf * k_row, axis=1, keepdims=True) + bias_ref[hs, BLK:BLK + 1]
            m = jnp.maximum(jnp.maximum(jnp.max(sc, axis=1, keepdims=True), s_self), sink)
            p = jnp.exp(sc - m)
            p_self = jnp.exp(s_self - m)
            denom = jnp.sum(p, axis=1, keepdims=True) + p_self + jnp.exp(sink - m)
            o = lax.dot_general(p.astype(BF16), st_v.astype(BF16), _NT, preferred_element_type=F32)
            o_ref[s, hs, :] = ((o + p_self * v_row) / denom).astype(o_ref.dtype)
            nk_ref[s, kv] = jnp.where(newest, kcol_ref[s, kv], pltpu.roll(st_k, buf - 1, axis=1))
            nv_ref[s, kv] = jnp.where(newest, kcol_ref[s, n_kv + kv], pltpu.roll(st_v, buf - 1, axis=1))


def _swa_attn_sample(q, kv_new, state_k_t, state_v_t, li, bias_rows, sinks, bb):
    b, n_heads, _ = q.shape
    n_kv, buf = state_k_t.shape[2], state_k_t.shape[4]
    kw = n_kv * HEAD_DIM
    assert buf == WINDOW and b % bb == 0
    kern = functools.partial(_swa_sample_kernel, n_kv=n_kv, bb=bb)
    st_spec = pl.BlockSpec((bb, None, n_kv, HEAD_DIM, buf), lambda i: (i, li, 0, 0, 0))
    new_spec = pl.BlockSpec((bb, n_kv, HEAD_DIM, buf), lambda i: (i, 0, 0, 0))
    new_shape = jax.ShapeDtypeStruct((b, n_kv, HEAD_DIM, buf), F32)
    o, nk, nv = pl.pallas_call(
        kern,
        grid=(b // bb,),
        in_specs=[pl.BlockSpec((bb, n_heads, HEAD_DIM), lambda i: (i, 0, 0)),
                  pl.BlockSpec((bb, 1, 2 * kw), lambda i: (i, 0, 0)),
                  pl.BlockSpec((bb, 2 * n_kv, HEAD_DIM, 1), lambda i: (i, 0, 0, 0)),
                  st_spec, st_spec,
                  pl.BlockSpec((n_heads, 2 * BLK), lambda i: (0, 0)),
                  pl.BlockSpec((n_heads, 1), lambda i: (0, 0))],
        out_specs=[pl.BlockSpec((bb, n_heads, HEAD_DIM), lambda i: (i, 0, 0)), new_spec, new_spec],
        out_shape=[jax.ShapeDtypeStruct((b, n_heads, HEAD_DIM), F32), new_shape, new_shape],
        compiler_params=_cparams(("arbitrary",)),
        name="swa_attn_sample",
    )(q, kv_new.reshape(b, 1, 2 * kw), kv_new.reshape(b, 2 * n_kv, HEAD_DIM, 1), state_k_t, state_v_t,
      bias_rows, sinks.reshape(-1, 1))
    return o.reshape(b, n_heads * HEAD_DIM).astype(BF16), nk, nv


def _block_rows(q_rows, n_kv_blocks, block_of_row):
    w = q_rows.shape[-1]
    tiled = jnp.tile(q_rows, (1, 1, n_kv_blocks))
    lane_block = np.arange(n_kv_blocks * w)[None, :] // w
    mask = jnp.asarray(lane_block == np.asarray(block_of_row)[:, None])
    return jnp.where(mask[None], tiled, jnp.zeros_like(tiled))


def kernel(x_prompt, x_sample, cache_k_diff, cache_v_diff, state_k_swa, state_v_swa, page_table,
           rel_bias, g_mix, g_ffn, g_final, w_qkv_diff, w_o_diff, lambda_diff, subln_diff,
           w_qkv_swa, b_qkv_swa, w_o_swa, sinks_swa, w_gate, w_up, w_down):
    batch, seq, d_model = x_prompt.shape
    dec_b = x_sample.shape[0]
    depth = g_mix.shape[0]
    n_pool, n_diff_layers, page, n_kv_diff, kd = cache_k_diff.shape
    n_heads_diff = w_o_diff.shape[1] // kd
    g_diff = n_heads_diff // n_kv_diff
    n_kv_swa = state_k_swa.shape[3]
    n_heads_swa = sinks_swa.shape[1]
    g_swa = n_heads_swa // n_kv_swa
    buf = state_k_swa.shape[2]
    q_diff = n_heads_diff * 2 * HEAD_DIM
    q_swa = n_heads_swa * HEAD_DIM
    kw_diff = n_kv_diff * kd
    kw_swa = n_kv_swa * HEAD_DIM

    bias_causal, bias_causal2, bias_band = _bias_tiles(rel_bias)
    diff_cols = np.array([2 * (r % n_heads_diff) + r // n_heads_diff for r in range(2 * n_heads_diff)])
    diff_row_map = np.arange(2 * n_heads_diff) // n_heads_diff
    near = bias_causal[:, 0, :][diff_cols]
    bias_rows_diff = jnp.concatenate([jnp.repeat(near[:, :BLK], n_kv_diff, axis=1), near[:, BLK:]], axis=1)
    bias_rows_swa = bias_band[:, 0, :]

    st_k = state_k_swa.transpose(0, 1, 3, 4, 2)
    st_v = state_v_swa.transpose(0, 1, 3, 4, 2)

    xp = x_prompt.reshape(batch * seq, d_model)
    xs = x_sample.reshape(dec_b, d_model)
    tm_p, tm_s = 512, dec_b
    zero_bias = jnp.zeros((1, w_qkv_diff.shape[2]), F32)
    wq_diff, wo_diff = w_qkv_diff.astype(BF16), w_o_diff.astype(BF16)
    wq_swa, wo_swa = w_qkv_swa.astype(BF16), w_o_swa.astype(BF16)
    wg, wu, wd = w_gate.astype(BF16), w_up.astype(BF16), w_down.astype(BF16)
    kds, vds, ksp, vsp, kss, vss = [], [], [], [], [], []
    new_cache = None

    for i in range(depth):
        li = i // 2
        g_m = g_mix[i].reshape(1, d_model)
        if i % 2 == 0:
            lam_init = 0.8 - 0.6 * math.exp(-0.3 * i)
            w_o = wo_diff
            subln = subln_diff[li].reshape(1, kd)
            qp, kvp_b, k_cache, v_cache = _qkv_proj_paged(xp, g_m, wq_diff, q_diff, n_kv_diff, tm_p, batch, li,
                                                          new_cache)
            new_cache = (k_cache, v_cache)
            qs, kvs_f, _ = _qkv_proj(xs, g_m, wq_diff, li, zero_bias, q_diff, tm_s)
            q_rows = qs.reshape(dec_b, n_kv_diff, g_diff, 2, HEAD_DIM).transpose(0, 3, 1, 2, 4)
            q_rows = _block_rows(q_rows.reshape(dec_b, 2 * n_heads_diff, HEAD_DIM), 2, diff_row_map)
            op, os_ = _diff_attn(qp, kvp_b, bias_causal2, q_rows, kvs_f[:, :kw_diff], kvs_f[:, kw_diff:],
                                 bias_rows_diff, cache_k_diff, cache_v_diff, li, page_table,
                                 lambda_diff[li], subln, lam_init, batch, seq, g_diff)
            kds.append(kvs_f[:, :kw_diff].reshape(dec_b, 1, n_kv_diff, kd))
            vds.append(kvs_f[:, kw_diff:].reshape(dec_b, 1, n_kv_diff, kd))
        else:
            w_o = wo_swa
            b_qkv = b_qkv_swa[li].reshape(1, -1)
            qp, kvp_f, kvp_b = _qkv_proj(xp, g_m, wq_swa, li, b_qkv, q_swa, tm_p)
            qs, kvs_f, _ = _qkv_proj(xs, g_m, wq_swa, li, b_qkv, q_swa, tm_s)
            op = _swa_attn_prompt(qp, kvp_b, bias_band, sinks_swa[li], batch, seq, n_kv_swa, g_swa)
            os_, nk, nv = _swa_attn_sample(qs.reshape(dec_b, n_heads_swa, HEAD_DIM).astype(F32), kvs_f,
                                           st_k, st_v, li, bias_rows_swa, sinks_swa[li], bb=8)
            tail = kvp_f.reshape(batch, seq, 2 * kw_swa)[:, seq - min(WINDOW, seq):]
            ksp.append(tail[..., :kw_swa].reshape(batch, -1, n_kv_swa, HEAD_DIM))
            vsp.append(tail[..., kw_swa:].reshape(batch, -1, n_kv_swa, HEAD_DIM))
            kss.append(nk.transpose(0, 3, 1, 2))
            vss.append(nv.transpose(0, 3, 1, 2))
        final = i == depth - 1
        g_f = g_ffn[i].reshape(1, d_model)
        gfin = g_final.reshape(1, d_model)
        xp = _post_attn(xp, op, w_o, li, g_f, wg, wu, wd, i, gfin, tm_p, final)
        xs = _post_attn(xs, os_, w_o, li, g_f, wg, wu, wd, i, gfin, tm_s, final)

    k_cache, v_cache = new_cache
    cache_shape = (batch, n_diff_layers, seq, n_kv_diff, kd)
    return (xp.reshape(batch, seq, d_model), xs.reshape(dec_b, 1, d_model),
            k_cache.reshape(cache_shape), v_cache.reshape(cache_shape),
            jnp.stack(kds, axis=1), jnp.stack(vds, axis=1),
            jnp.stack(ksp, axis=1), jnp.stack(vsp, axis=1), jnp.stack(kss, axis=1), jnp.stack(vss, axis=1))
```
